```python
import functools
import jax, jax.numpy as jnp
from jax import lax
import numpy as np

D_MODEL = 1024
BATCH = 2
SEQ = 16384
DEPTH = 1
DEC_BATCH = 128
DEC_SEQ = 1
PAST_LEN = 8192
PAGE_SIZE = 128

D_MIX = D_MODEL
GLA_WIDTH = D_MIX // 2
GLA_HEADS = 4
GLA_QK = GLA_WIDTH // 2
GLA_DK = GLA_QK // GLA_HEADS
GLA_DV = GLA_WIDTH // GLA_HEADS
GLA_GATE_RANK = 16
GLA_TAU = 16.0
GLA_CHUNK = 64
NSA_WIDTH = D_MIX - GLA_WIDTH
NSA_HEADS = 8
NSA_HEAD_DIM = NSA_WIDTH // NSA_HEADS
NSA_KV_HEADS = 2
NSA_GROUP = NSA_HEADS // NSA_KV_HEADS
KV_W = NSA_KV_HEADS * NSA_HEAD_DIM
CMP_BLOCK = 64
SEL_BLOCK = CMP_BLOCK
TOP_N = 16
WINDOW = 512
Q_BLOCK = 128
ROT_DIM = NSA_HEAD_DIM // 4
ROPE_THETA = 500000.0
D_FF = 2816
CONV_W = 3
PLE_DIM = 256
EPS = 1e-6
SPLITS = (GLA_QK, GLA_QK, GLA_WIDTH, GLA_WIDTH, GLA_GATE_RANK,
          NSA_WIDTH, KV_W, KV_W, KV_W, KV_W, KV_W, KV_W, 3 * NSA_HEADS)
D_IN = sum(SPLITS)

kernel_name = 'hymba_gla_nsa_convffn_step'


def rmsnorm(x, g):
    xf = x.astype(jnp.float32)
    y = xf * lax.rsqrt(jnp.mean(xf * xf, axis=-1, keepdims=True) + EPS) * g.astype(jnp.float32)
    return y.astype(x.dtype)


def split_cols(z):
    out, start = [], 0
    for n in SPLITS:
        out.append(z[..., start:start + n])
        start += n
    return out


def rope(x, pos):
    half = ROT_DIM // 2
    inv = ROPE_THETA ** (-jnp.arange(half, dtype=jnp.float32) / half)
    ang = pos.astype(jnp.float32)[:, None] * inv[None, :]
    cos = jnp.cos(ang)[None, :, None, :]
    sin = jnp.sin(ang)[None, :, None, :]
    xf = x.astype(jnp.float32)
    x1 = xf[..., :half]
    x2 = xf[..., half:ROT_DIM]
    out = jnp.concatenate([x1 * cos - x2 * sin, x2 * cos + x1 * sin, xf[..., ROT_DIM:]], axis=-1)
    return out.astype(x.dtype)


def masked_softmax(s, mask):
    s = jnp.where(mask, s.astype(jnp.float32), -jnp.inf)
    m = jnp.max(s, axis=-1, keepdims=True)
    m = jnp.where(jnp.isfinite(m), m, 0.0)
    e = jnp.where(mask, jnp.exp(s - m), 0.0)
    return e / jnp.maximum(jnp.sum(e, axis=-1, keepdims=True), 1e-30)


def gla_chunked(q, k, v, log_a, h0):
    B, T, H, dk = q.shape
    dv = v.shape[-1]
    C = min(GLA_CHUNK, T)
    pad = (-T) % C
    n = (T + pad) // C

    def blocks(a):
        a = jnp.pad(a.astype(jnp.float32), ((0, 0), (0, pad), (0, 0), (0, 0)))
        return a.reshape(B, n, C, H, a.shape[-1]).transpose(1, 0, 3, 2, 4)

    qb, kb, vb, lb = blocks(q), blocks(k), blocks(v), blocks(log_a)
    bcum = jnp.cumsum(lb, axis=3)
    causal = jnp.tril(jnp.ones((C, C), dtype=bool))

    def step(h, inp):
        qc, kc, vc, bc = inp
        qe = qc * jnp.exp(bc)
        ke = kc * jnp.exp(-bc)
        att = jnp.where(causal, jnp.einsum('bhid,bhjd->bhij', qe, ke), 0.0)
        o = jnp.einsum('bhij,bhjv->bhiv', att, vc) + jnp.einsum('bhid,bhdv->bhiv', qe, h)
        blast = bc[:, :, -1, :]
        h = jnp.exp(blast)[..., None] * h + jnp.einsum(
            'bhjd,bhjv->bhdv', kc * jnp.exp(blast[:, :, None, :] - bc), vc)
        return h, o

    h, o = lax.scan(step, h0.astype(jnp.float32), (qb, kb, vb, bcum))
    o = o.transpose(1, 0, 3, 2, 4).reshape(B, n * C, H, dv)[:, :T]
    return o, h


def compress(k, w):
    B, T = k.shape[:2]
    kb = k.reshape(B, T // CMP_BLOCK, CMP_BLOCK, NSA_KV_HEADS, NSA_HEAD_DIM)
    return jnp.einsum('bnlgd,lg->bngd', kb, w.astype(k.dtype))


def nsa_core(q, q_pos, kc, vc, get_sel, kw, vw, kw_pos, gates):
    B, Q = q.shape[:2]
    NC = kc.shape[1]
    qg = q.reshape(B, Q, NSA_KV_HEADS, NSA_GROUP, NSA_HEAD_DIM)
    t = q_pos[:, None]
    blk = jnp.arange(NC, dtype=jnp.int32)[None, :]
    cmask = blk * CMP_BLOCK + (CMP_BLOCK - 1) <= t
    p_cmp = masked_softmax(jnp.einsum('bqgrd,bngd->bgrqn', qg, kc), cmask[None, None, None])
    o_cmp = jnp.einsum('bgrqn,bngd->bqgrd', p_cmp.astype(vc.dtype), vc)
    cur = t // SEL_BLOCK
    forced = (blk == 0) | (blk == cur) | (blk == cur - 1)
    score = jnp.where(forced, float(NSA_GROUP + 1), jnp.sum(p_cmp, axis=2))
    score = jnp.where(blk > cur, -1.0, score)
    _, idx = lax.top_k(score, min(TOP_N, NC))
    ks, vs = get_sel(idx)
    n_sel = idx.shape[-1]
    spos = idx[..., None] * SEL_BLOCK + jnp.arange(SEL_BLOCK, dtype=jnp.int32)
    smask = (spos <= q_pos[None, None, :, None, None]).reshape(
        B, NSA_KV_HEADS, 1, Q, n_sel * SEL_BLOCK)
    s = jnp.einsum('bqgrd,bgqnld->bgrqnl', qg, ks).reshape(
        B, NSA_KV_HEADS, NSA_GROUP, Q, n_sel * SEL_BLOCK)
    p = masked_softmax(s, smask).reshape(B, NSA_KV_HEADS, NSA_GROUP, Q, n_sel, SEL_BLOCK)
    o_slc = jnp.einsum('bgrqnl,bgqnld->bqgrd', p.astype(vs.dtype), vs)
    kp = kw_pos[None, :]
    wmask = (kp >= 0) & (kp <= t) & (t - kp < WINDOW)
    p = masked_softmax(jnp.einsum('bqgrd,bwgd->bgrqw', qg, kw), wmask[None, None, None])
    o_win = jnp.einsum('bgrqw,bwgd->bqgrd', p.astype(vw.dtype), vw)
    o = jnp.stack([o_cmp, o_slc, o_win], axis=-1).reshape(B, Q, NSA_HEADS, NSA_HEAD_DIM, 3)
    return jnp.sum(o * gates[:, :, :, None, :], axis=-1)


def nsa_prompt(q, kc_r, vc_r, ks_r, vs_r, kw_r, vw_r, gates, w_ck, w_cv):
    B, T = q.shape[:2]
    kc = compress(kc_r, w_ck)
    vc = compress(vc_r, w_cv)
    nb = T // SEL_BLOCK
    kb = ks_r.reshape(B, nb, SEL_BLOCK, NSA_KV_HEADS, NSA_HEAD_DIM)
    vb = vs_r.reshape(B, nb, SEL_BLOCK, NSA_KV_HEADS, NSA_HEAD_DIM)
    bi = jnp.arange(B)[:, None, None, None]
    gi = jnp.arange(NSA_KV_HEADS)[None, :, None, None]

    def get_sel(idx):
        return kb[bi, idx, :, gi], vb[bi, idx, :, gi]

    wpad = ((0, 0), (WINDOW, 0), (0, 0), (0, 0))
    kw_pad = jnp.pad(kw_r, wpad)
    vw_pad = jnp.pad(vw_r, wpad)
    nq = T // Q_BLOCK

    def to_blocks(a):
        return a.reshape(B, nq, Q_BLOCK, *a.shape[2:]).swapaxes(0, 1)

    def one_block(args):
        qb, gb, c = args
        q0 = c * Q_BLOCK
        kw = lax.dynamic_slice_in_dim(kw_pad, q0, WINDOW + Q_BLOCK, axis=1)
        vw = lax.dynamic_slice_in_dim(vw_pad, q0, WINDOW + Q_BLOCK, axis=1)
        kw_pos = q0 - WINDOW + jnp.arange(WINDOW + Q_BLOCK, dtype=jnp.int32)
        q_pos = q0 + jnp.arange(Q_BLOCK, dtype=jnp.int32)
        return nsa_core(qb, q_pos, kc, vc, get_sel, kw, vw, kw_pos, gb)

    out = lax.map(one_block, (to_blocks(q), to_blocks(gates), jnp.arange(nq, dtype=jnp.int32)))
    out = out.swapaxes(0, 1).reshape(B, T, NSA_HEADS, NSA_HEAD_DIM)
    nw = min(WINDOW, T)
    return out, (kc_r, vc_r, ks_r, vs_r, kw_r[:, T - nw:], vw_r[:, T - nw:])


def nsa_sample(q, kc_r, vc_r, ks_r, vs_r, kw_r, vw_r, gates, w_ck, w_cv,
               pool_kc, pool_vc, pool_ks, pool_vs, buf_kw, buf_vw, page_table):
    B, T = q.shape[:2]
    past_len = page_table.shape[1] * PAGE_SIZE

    def gather_past(pool):
        return pool[page_table].reshape(B, past_len, NSA_KV_HEADS, NSA_HEAD_DIM)

    tpad = (-T) % CMP_BLOCK

    def padn(a):
        return jnp.pad(a, ((0, 0), (0, tpad), (0, 0), (0, 0)))

    kc = jnp.concatenate([compress(gather_past(pool_kc), w_ck), compress(padn(kc_r), w_ck)], axis=1)
    vc = jnp.concatenate([compress(gather_past(pool_vc), w_cv), compress(padn(vc_r), w_cv)], axis=1)
    n_past_blk = past_len // SEL_BLOCK
    n_new_blk = (T + tpad) // SEL_BLOCK
    bpp = PAGE_SIZE // SEL_BLOCK
    pool_kb = pool_ks.reshape(-1, SEL_BLOCK, NSA_KV_HEADS, NSA_HEAD_DIM)
    pool_vb = pool_vs.reshape(-1, SEL_BLOCK, NSA_KV_HEADS, NSA_HEAD_DIM)
    new_kb = padn(ks_r).reshape(B, n_new_blk, SEL_BLOCK, NSA_KV_HEADS, NSA_HEAD_DIM)
    new_vb = padn(vs_r).reshape(B, n_new_blk, SEL_BLOCK, NSA_KV_HEADS, NSA_HEAD_DIM)
    bi = jnp.arange(B)[:, None, None, None]
    gi = jnp.arange(NSA_KV_HEADS)[None, :, None, None]

    def get_sel(idx):
        in_past = (idx < n_past_blk)[..., None, None]
        pi = jnp.clip(idx, 0, n_past_blk - 1)
        phys = page_table[bi, pi // bpp] * bpp + pi % bpp
        ni = jnp.clip(idx - n_past_blk, 0, n_new_blk - 1)
        ks = jnp.where(in_past, pool_kb[phys, :, gi], new_kb[bi, ni, :, gi])
        vs = jnp.where(in_past, pool_vb[phys, :, gi], new_vb[bi, ni, :, gi])
        return ks, vs

    wb = buf_kw.shape[1]
    kw = jnp.concatenate([buf_kw, kw_r], axis=1)
    vw = jnp.concatenate([buf_vw, vw_r], axis=1)
    kw_pos = past_len - wb + jnp.arange(wb + T, dtype=jnp.int32)
    q_pos = past_len + jnp.arange(T, dtype=jnp.int32)
    out = nsa_core(q, q_pos, kc, vc, get_sel, kw, vw, kw_pos, gates)
    nw = min(WINDOW, wb + T)
    return out, (kc_r, vc_r, ks_r, vs_r, kw[:, wb + T - nw:], vw[:, wb + T - nw:])


def conv_ffn(x, prev, w_up, w_conv, b_conv, w_down):
    u = x @ w_up
    T = u.shape[1]
    ext = jnp.concatenate([prev.astype(u.dtype), u], axis=1)
    c = b_conv
    for j in range(CONV_W):
        c = c + ext[:, j:j + T] * w_conv[j]
    a, b = jnp.split(c, 2, axis=-1)
    return (jax.nn.silu(a) * b) @ w_down, ext[:, ext.shape[1] - (CONV_W - 1):]


def trunk_layer(x, pe, pos, gla_h0, conv_prev, nsa_fn, g_attn, w_in, w_gla_gate, b_gla_gate,
                g_gla_out, b_nsa_gate, w_o, g_ffn, w_up, w_conv, b_conv, w_down,
                g_ple, w_ple, w_ple_gate):
    B, T, _ = x.shape
    xn = rmsnorm(x, g_attn)
    (gq, gk, gv, gr, glr, nq, nkc, nvc, nks, nvs, nkw, nvw, ng) = split_cols(xn @ w_in)
    q = gq.reshape(B, T, GLA_HEADS, GLA_DK) * (GLA_DK ** -0.5)
    k = gk.reshape(B, T, GLA_HEADS, GLA_DK)
    v = gv.reshape(B, T, GLA_HEADS, GLA_DV)
    log_a = jax.nn.log_sigmoid((glr @ w_gla_gate + b_gla_gate).astype(jnp.float32)) / GLA_TAU
    o, h_gla = gla_chunked(q, k, v, log_a.reshape(B, T, GLA_HEADS, GLA_DK), gla_h0)
    o_gla = rmsnorm(o.astype(x.dtype), g_gla_out.reshape(GLA_HEADS, GLA_DV)).reshape(
        B, T, GLA_WIDTH) * jax.nn.silu(gr)
    kvs = (B, T, NSA_KV_HEADS, NSA_HEAD_DIM)
    qn = rope(nq.reshape(B, T, NSA_HEADS, NSA_HEAD_DIM), pos) * (NSA_HEAD_DIM ** -0.5)
    gates = jax.nn.sigmoid(ng + b_nsa_gate).reshape(B, T, NSA_HEADS, 3)
    o_nsa, nsa_new = nsa_fn(qn, rope(nkc.reshape(kvs), pos), nvc.reshape(kvs),
                            rope(nks.reshape(kvs), pos), nvs.reshape(kvs),
                            rope(nkw.reshape(kvs), pos), nvw.reshape(kvs), gates)
    h = x + jnp.concatenate([o_gla, o_nsa.reshape(B, T, NSA_WIDTH)], axis=-1) @ w_o
    f, conv_new = conv_ffn(rmsnorm(h, g_ffn), conv_prev, w_up, w_conv, b_conv, w_down)
    h = h + f
    h = h + (pe @ w_ple) * jax.nn.sigmoid(rmsnorm(h, g_ple) @ w_ple_gate)
    return h, (*nsa_new, h_gla.astype(gla_h0.dtype), conv_new)


def setup_inputs(seed: int = 0) -> dict:
    key = jax.random.key(seed)
    keys = jax.random.split(key, 32)

    def nrm(i, shape, scale):
        return jax.random.normal(keys[i], shape, jnp.float32) * scale

    n_pages = PAST_LEN // PAGE_SIZE
    n_pool = (DEC_BATCH * n_pages * 5) // 4
    win_buf = min(WINDOW, PAST_LEN)
    kv_pool = (DEPTH, n_pool, PAGE_SIZE, NSA_KV_HEADS, NSA_HEAD_DIM)
    win = (DEPTH, DEC_BATCH, win_buf, NSA_KV_HEADS, NSA_HEAD_DIM)
    page_table = jax.random.permutation(keys[12], n_pool)[:DEC_BATCH * n_pages].reshape(
        DEC_BATCH, n_pages).astype(jnp.int32)
    return {
        'x_prompt': nrm(0, (BATCH, SEQ, D_MODEL), 1.0),
        'x_sample': nrm(1, (DEC_BATCH, DEC_SEQ, D_MODEL), 1.0),
        'p_prompt': nrm(2, (DEPTH, BATCH, SEQ, PLE_DIM), 1.0),
        'p_sample': nrm(3, (DEPTH, DEC_BATCH, DEC_SEQ, PLE_DIM), 1.0),
        'cache_k_cmp': nrm(4, kv_pool, 1.0),
        'cache_v_cmp': nrm(5, kv_pool, 1.0),
        'cache_k_slc': nrm(6, kv_pool, 1.0),
        'cache_v_slc': nrm(7, kv_pool, 1.0),
        'cache_k_win': nrm(8, win, 1.0),
        'cache_v_win': nrm(9, win, 1.0),
        'state_gla': nrm(10, (DEPTH, DEC_BATCH, GLA_HEADS, GLA_DK, GLA_DV), 0.5),
        'state_conv': nrm(11, (DEPTH, DEC_BATCH, CONV_W - 1, 2 * D_FF), 1.0),
        'page_table': page_table,
        'g_attn': 1.0 + nrm(13, (DEPTH, D_MODEL), 0.02),
        'w_in': nrm(14, (DEPTH, D_MODEL, D_IN), D_MODEL ** -0.5),
        'w_gla_gate': nrm(15, (DEPTH, GLA_GATE_RANK, GLA_QK), GLA_GATE_RANK ** -0.5),
        'b_gla_gate': nrm(16, (DEPTH, GLA_QK), 0.1),
        'g_gla_out': 1.0 + nrm(17, (DEPTH, GLA_WIDTH), 0.02),
        'b_nsa_gate': nrm(18, (DEPTH, 3 * NSA_HEADS), 0.1),
        'w_cmp_k': (1.0 + nrm(19, (DEPTH, CMP_BLOCK, NSA_KV_HEADS), 0.1)) / CMP_BLOCK,
        'w_cmp_v': (1.0 + nrm(20, (DEPTH, CMP_BLOCK, NSA_KV_HEADS), 0.1)) / CMP_BLOCK,
        'w_o': nrm(21, (DEPTH, D_MIX, D_MODEL), D_MIX ** -0.5),
        'g_ffn': 1.0 + nrm(22, (DEPTH, D_MODEL), 0.02),
        'w_up': nrm(23, (DEPTH, D_MODEL, 2 * D_FF), D_MODEL ** -0.5),
        'w_conv': nrm(24, (DEPTH, CONV_W, 2 * D_FF), CONV_W ** -0.5),
        'b_conv': nrm(25, (DEPTH, 2 * D_FF), 0.02),
        'w_down': nrm(26, (DEPTH, D_FF, D_MODEL), D_FF ** -0.5),
        'g_ple': 1.0 + nrm(27, (DEPTH, D_MODEL), 0.02),
        'w_ple': nrm(28, (DEPTH, PLE_DIM, D_MODEL), PLE_DIM ** -0.5),
        'w_ple_gate': nrm(29, (DEPTH, D_MODEL, D_MODEL), D_MODEL ** -0.5),
        'g_final': 1.0 + nrm(30, (D_MODEL,), 0.02),
    }


def reference(x_prompt, x_sample, p_prompt, p_sample, cache_k_cmp, cache_v_cmp, cache_k_slc,
              cache_v_slc, cache_k_win, cache_v_win, state_gla, state_conv, page_table,
              g_attn, w_in, w_gla_gate, b_gla_gate, g_gla_out, b_nsa_gate, w_cmp_k, w_cmp_v,
              w_o, g_ffn, w_up, w_conv, b_conv, w_down, g_ple, w_ple, w_ple_gate, g_final):
    B, S = x_prompt.shape[:2]
    Sd = x_sample.shape[1]
    pos_p = jnp.arange(S, dtype=jnp.int32)
    pos_s = PAST_LEN + jnp.arange(Sd, dtype=jnp.int32)
    hp, hs = x_prompt, x_sample
    st_p = [[] for _ in range(8)]
    st_s = [[] for _ in range(8)]
    for i in range(DEPTH):
        lw = (g_attn[i], w_in[i], w_gla_gate[i], b_gla_gate[i], g_gla_out[i], b_nsa_gate[i],
              w_o[i], g_ffn[i], w_up[i], w_conv[i], b_conv[i], w_down[i],
              g_ple[i], w_ple[i], w_ple_gate[i])
        nsa_p = functools.partial(nsa_prompt, w_ck=w_cmp_k[i], w_cv=w_cmp_v[i])
        hp, new_p = trunk_layer(
            hp, p_prompt[i], pos_p,
            jnp.zeros((B, GLA_HEADS, GLA_DK, GLA_DV), jnp.float32),
            jnp.zeros((B, CONV_W - 1, 2 * D_FF), hp.dtype), nsa_p, *lw)
        nsa_s = functools.partial(
            nsa_sample, w_ck=w_cmp_k[i], w_cv=w_cmp_v[i],
            pool_kc=cache_k_cmp[i], pool_vc=cache_v_cmp[i],
            pool_ks=cache_k_slc[i], pool_vs=cache_v_slc[i],
            buf_kw=cache_k_win[i], buf_vw=cache_v_win[i], page_table=page_table)
        hs, new_s = trunk_layer(hs, p_sample[i], pos_s, state_gla[i], state_conv[i], nsa_s, *lw)
        for j in range(8):
            st_p[j].append(new_p[j])
            st_s[j].append(new_s[j])
    sp = [jnp.stack(a, axis=0) for a in st_p]
    ss = [jnp.stack(a, axis=0) for a in st_s]
    y_prompt = rmsnorm(hp, g_final)
    y_sample = rmsnorm(hs, g_final)
    return (y_prompt, y_sample, sp[0], sp[1], sp[2], sp[3], sp[4], sp[5], sp[6], sp[7],
            ss[0], ss[1], ss[2], ss[3], ss[4], ss[5], ss[6], ss[7])
```

```python
import functools

import jax
import jax.numpy as jnp
from jax import lax
from jax.experimental import pallas as pl
from jax.experimental.pallas import tpu as pltpu

F32 = jnp.float32
BF16 = jnp.bfloat16

D_MODEL = 1024
PAGE_SIZE = 128
GLA_WIDTH = 512
GLA_HEADS = 4
GLA_QK = 256
GLA_DK = 64
GLA_DV = 128
GLA_GATE_RANK = 16
GLA_TAU = 16.0
GLA_CHUNK = 64
NSA_WIDTH = 512
NSA_HEADS = 8
NSA_HEAD_DIM = 64
NSA_KV_HEADS = 2
NSA_GROUP = 4
KV_W = 128
CMP_BLOCK = 64
SEL_BLOCK = 64
TOP_N = 16
WINDOW = 512
ROT_DIM = 16
ROPE_THETA = 500000.0
D_FF = 2816
CONV_W = 3
PLE_DIM = 256
EPS = 1e-6

LANE = 128
SUBLANE = 8
NEG = -1e30
FF_CHUNK = 256
N_FF_CHUNKS = D_FF // FF_CHUNK
VMEM_LIMIT = 56 * 1024 * 1024

C_GQ, C_GK, C_GV, C_GR, C_NQ = 0, 256, 512, 1024, 1536
C_KC, C_VC, C_KS, C_VS, C_KW, C_VW, C_MISC = 2048, 2176, 2304, 2432, 2560, 2688, 2816
D_PACK = 2944
GATE_LANE0 = GLA_GATE_RANK


def _nt(a, b):
    return lax.dot_general(a, b, (((1,), (1,)), ((), ())), preferred_element_type=F32)


def _tn(a, b):
    return lax.dot_general(a, b, (((0,), (0,)), ((), ())), preferred_element_type=F32)


def _mm(a, b):
    return jnp.dot(a, b, preferred_element_type=F32)


def _rms(x, g):
    return x * lax.rsqrt(jnp.mean(x * x, axis=-1, keepdims=True) + EPS) * g


def _sigmoid(x):
    return 1.0 / (1.0 + jnp.exp(-x))


def _group_rows(qrow, g):
    parts = [qrow[:, (g * NSA_GROUP + r) * NSA_HEAD_DIM:(g * NSA_GROUP + r + 1) * NSA_HEAD_DIM]
             for r in range(NSA_GROUP)]
    parts.append(jnp.zeros((SUBLANE - NSA_GROUP, NSA_HEAD_DIM), F32))
    return jnp.concatenate(parts, axis=0).astype(BF16)


def _const_spec(shape):
    nd = len(shape)
    return pl.BlockSpec(shape, lambda *_: (0,) * nd, pipeline_mode=pl.Buffered(1))


def _inproj_kernel(x_ref, g_ref, w_ref, wgate_ref, bgate_ref, bmisc_ref, cos_ref, sa_ref, sb_ref,
                   wck_ref, wcv_ref, *out_refs, compress):
    (gqk_ref, gv_ref, gr_ref, la_ref, nq_ref, kc_ref, vc_ref, ks_ref, vs_ref, kw_ref, vw_ref,
     misc_ref) = out_refs[:12]
    xn = _rms(x_ref[...], g_ref[...]).astype(BF16)

    def proj(lo, hi):
        return _mm(xn, w_ref[:, lo:hi])

    cos, sa, sb = cos_ref[...], sa_ref[...], sb_ref[...]

    def rope(z):
        return z * cos + pltpu.roll(z, LANE - ROT_DIM // 2, 1) * sa + pltpu.roll(z, ROT_DIM // 2, 1) * sb

    gqk_ref[...] = proj(C_GQ, C_GV)
    gv_ref[...] = proj(C_GV, C_GR)
    gr_ref[...] = proj(C_GR, C_NQ)
    for j in range(NSA_WIDTH // LANE):
        z = proj(C_NQ + j * LANE, C_NQ + (j + 1) * LANE)
        nq_ref[:, j * LANE:(j + 1) * LANE] = (rope(z) * (NSA_HEAD_DIM ** -0.5)).astype(BF16)
    kc = rope(proj(C_KC, C_VC))
    vc = proj(C_VC, C_KS)
    kc_ref[...] = kc
    vc_ref[...] = vc
    ks_ref[...] = rope(proj(C_KS, C_VS))
    vs_ref[...] = proj(C_VS, C_KW)
    kw_ref[...] = rope(proj(C_KW, C_VW))
    vw_ref[...] = proj(C_VW, C_MISC)
    z = proj(C_MISC, D_PACK)
    misc_ref[...] = _sigmoid(z + bmisc_ref[...])
    pre = _mm(z.astype(BF16), wgate_ref[...]) + bgate_ref[...]
    la_ref[...] = (jnp.minimum(pre, 0.0) - jnp.log(1.0 + jnp.exp(-jnp.abs(pre)))) * (1.0 / GLA_TAU)
    if compress:
        kcc_ref, vcc_ref = out_refs[12:]
        nb = kc.shape[0] // CMP_BLOCK
        kcc_ref[...] = jnp.sum(kc.reshape(nb, CMP_BLOCK, KV_W) * wck_ref[...][None], axis=1)
        vcc_ref[...] = jnp.sum(vc.reshape(nb, CMP_BLOCK, KV_W) * wcv_ref[...][None], axis=1)


def _inproj(x2d, tabs, pw, *, tm, compress):
    m = x2d.shape[0]
    n_tab = tabs[0].shape[0] // tm
    row = lambda n: pl.BlockSpec((tm, n), lambda i: (i, 0))
    tab = pl.BlockSpec((tm, LANE), lambda i: (i % n_tab, 0))
    widths = [2 * GLA_QK, GLA_WIDTH, GLA_WIDTH, GLA_QK, NSA_WIDTH] + [KV_W] * 6 + [LANE]
    dtypes = [F32, F32, F32, F32, BF16] + [F32] * 7
    out_shape = [jax.ShapeDtypeStruct((m, n), d) for n, d in zip(widths, dtypes)]
    out_specs = [row(n) for n in widths]
    if compress:
        nb = tm // CMP_BLOCK
        out_shape += [jax.ShapeDtypeStruct((m // CMP_BLOCK, KV_W), F32)] * 2
        out_specs += [pl.BlockSpec((nb, KV_W), lambda i: (i, 0))] * 2
    return pl.pallas_call(
        functools.partial(_inproj_kernel, compress=compress),
        grid=(m // tm,),
        in_specs=[row(D_MODEL), _const_spec((1, D_MODEL)), _const_spec((D_MODEL, D_PACK)),
                  _const_spec((LANE, GLA_QK)), _const_spec((1, GLA_QK)), _const_spec((1, LANE)),
                  tab, tab, tab, _const_spec((CMP_BLOCK, KV_W)), _const_spec((CMP_BLOCK, KV_W))],
        out_specs=out_specs, out_shape=out_shape,
        compiler_params=pltpu.CompilerParams(dimension_semantics=("arbitrary",),
                                             vmem_limit_bytes=VMEM_LIMIT),
        name="inproj",
    )(x2d, pw["g_attn"], pw["w_in"], pw["w_gate"], pw["b_gate"], pw["b_misc"], *tabs,
      pw["w_ck"], pw["w_cv"])


def _gla_kernel(qk_ref, v_ref, r_ref, la_ref, g_ref, o_ref, hout_ref, ht_scr, o_scr, *, n_chunks):
    t = pl.program_id(1)

    @pl.when(t == 0)
    def _():
        ht_scr[...] = jnp.zeros_like(ht_scr)

    ri = lax.broadcasted_iota(jnp.int32, (GLA_CHUNK, GLA_CHUNK), 0)
    ci = lax.broadcasted_iota(jnp.int32, (GLA_CHUNK, GLA_CHUNK), 1)
    causal = ri >= ci
    tril = causal.astype(F32)

    def chunk(c, carry):
        r0 = pl.multiple_of(c * GLA_CHUNK, GLA_CHUNK)
        rows = pl.ds(r0, GLA_CHUNK)
        bc = jnp.dot(tril, la_ref[rows, :], preferred_element_type=F32,
                     precision=lax.Precision.HIGHEST)
        blast = bc[GLA_CHUNK - 1:GLA_CHUNK, :]
        q = qk_ref[rows, 0:GLA_QK]
        k = qk_ref[rows, GLA_QK:2 * GLA_QK]
        qe = (q * jnp.exp(bc) * (GLA_DK ** -0.5)).astype(BF16)
        ke = (k * jnp.exp(-bc)).astype(BF16)
        kd = (k * jnp.exp(blast - bc)).astype(BF16)
        decay = jnp.exp(blast)
        for h in range(GLA_HEADS):
            ks = slice(h * GLA_DK, (h + 1) * GLA_DK)
            vh = v_ref[rows, h * GLA_DV:(h + 1) * GLA_DV].astype(BF16)
            ht = ht_scr[h]
            att = jnp.where(causal, _nt(qe[:, ks], ke[:, ks]), 0.0).astype(BF16)
            o_scr[rows, h * GLA_DV:(h + 1) * GLA_DV] = _mm(att, vh) + _nt(qe[:, ks], ht.astype(BF16))
            ht_scr[h] = ht * decay[:, ks] + _tn(vh, kd[:, ks])
        return carry

    lax.fori_loop(0, n_chunks, chunk, 0)
    for h in range(GLA_HEADS):
        hs = slice(h * GLA_DV, (h + 1) * GLA_DV)
        r = r_ref[:, hs]
        o_ref[:, hs] = _rms(o_scr[:, hs], g_ref[:, hs]) * (r * _sigmoid(r))

    @pl.when(t == pl.num_programs(1) - 1)
    def _():
        for h in range(GLA_HEADS):
            hout_ref[0, h] = ht_scr[h].T


def _gla_prompt(gqk, gv, gr, la, g_out, *, batch, seq, tg):
    nt = seq // tg
    row = lambda n: pl.BlockSpec((tg, n), lambda b, t: (b * nt + t, 0))
    return pl.pallas_call(
        functools.partial(_gla_kernel, n_chunks=tg // GLA_CHUNK),
        grid=(batch, nt),
        in_specs=[row(2 * GLA_QK), row(GLA_WIDTH), row(GLA_WIDTH), row(GLA_QK),
                  pl.BlockSpec((1, GLA_WIDTH), lambda b, t: (0, 0))],
        out_specs=[row(GLA_WIDTH),
                   pl.BlockSpec((1, GLA_HEADS, GLA_DK, GLA_DV), lambda b, t: (b, 0, 0, 0))],
        out_shape=[jax.ShapeDtypeStruct((batch * seq, GLA_WIDTH), F32),
                   jax.ShapeDtypeStruct((batch, GLA_HEADS, GLA_DK, GLA_DV), F32)],
        scratch_shapes=[pltpu.VMEM((GLA_HEADS, GLA_DV, GLA_DK), F32),
                        pltpu.VMEM((tg, GLA_WIDTH), F32)],
        compiler_params=pltpu.CompilerParams(dimension_semantics=("arbitrary", "arbitrary"),
                                             vmem_limit_bytes=VMEM_LIMIT),
        name="gla_prompt",
    )(gqk, gv, gr, la, g_out)


def _gla_step_kernel(qt_ref, kt_ref, at_ref, v_ref, r_ref, g_ref, h0_ref, o_ref, h_ref, *, bb):
    for b in range(bb):
        for h in range(GLA_HEADS):
            ds_ = slice(h * GLA_DK, (h + 1) * GLA_DK)
            vs_ = slice(h * GLA_DV, (h + 1) * GLA_DV)
            qcol = qt_ref[0, ds_, b:b + 1] * (GLA_DK ** -0.5)
            kcol = kt_ref[0, ds_, b:b + 1]
            acol = jnp.exp(at_ref[0, ds_, b:b + 1])
            hn = acol * h0_ref[b, h] + kcol * v_ref[b:b + 1, vs_]
            h_ref[b, h] = hn
            o = jnp.sum(qcol * hn, axis=0, keepdims=True)
            r = r_ref[b:b + 1, vs_]
            o_ref[b:b + 1, vs_] = _rms(o, g_ref[:, vs_]) * (r * _sigmoid(r))


def _gla_step(gqk, gv, gr, la, g_out, h0, *, bb):
    n = gqk.shape[0]
    ns = n // bb

    def cols(a):
        return a.reshape(ns, bb, GLA_QK).transpose(0, 2, 1)

    colspec = pl.BlockSpec((1, GLA_QK, bb), lambda i: (i, 0, 0))
    row = pl.BlockSpec((bb, GLA_WIDTH), lambda i: (i, 0))
    st = pl.BlockSpec((bb, GLA_HEADS, GLA_DK, GLA_DV), lambda i: (i, 0, 0, 0))
    return pl.pallas_call(
        functools.partial(_gla_step_kernel, bb=bb),
        grid=(ns,),
        in_specs=[colspec, colspec, colspec, row, row,
                  pl.BlockSpec((1, GLA_WIDTH), lambda i: (0, 0)), st],
        out_specs=[row, st],
        out_shape=[jax.ShapeDtypeStruct((n, GLA_WIDTH), F32),
                   jax.ShapeDtypeStruct(h0.shape, F32)],
        compiler_params=pltpu.CompilerParams(dimension_semantics=("arbitrary",)),
        name="gla_step",
    )(cols(gqk[:, :GLA_QK]), cols(gqk[:, GLA_QK:]), cols(la), gv, gr, g_out, h0)


def _masked_softmax(s, mask):
    s = jnp.where(mask, s, NEG)
    m = jnp.max(s, axis=-1, keepdims=True)
    e = jnp.where(mask, jnp.exp(s - m), 0.0)
    return e / jnp.maximum(jnp.sum(e, axis=-1, keepdims=True), 1e-30)


def _top_blocks(score, blk, n_pick, n_blk):
    sel = jnp.zeros(score.shape, jnp.bool_)
    for _ in range(n_pick):
        m = jnp.max(score, axis=-1, keepdims=True)
        j = jnp.min(jnp.where(score == m, blk, n_blk), axis=-1, keepdims=True)
        pick = blk == j
        sel = jnp.logical_or(sel, pick)
        score = jnp.where(pick, -2.0, score)
    return sel


def _cmp_kernel(q_ref, kc_ref, vc_ref, gate_ref, o_ref, selt_ref, *, tq, nc):
    q0 = pl.program_id(1) * tq
    t = q0 + lax.broadcasted_iota(jnp.int32, (tq, 1), 0)
    blk = lax.broadcasted_iota(jnp.int32, (1, nc), 1)
    cmask = blk * CMP_BLOCK + (CMP_BLOCK - 1) <= t
    cur = t // SEL_BLOCK
    forced = (blk == 0) | (blk == cur) | (blk == cur - 1)
    future = blk > cur
    kc = kc_ref[0].astype(BF16)
    vc = vc_ref[0].astype(BF16)
    for g in range(NSA_KV_HEADS):
        gl = slice(g * NSA_HEAD_DIM, (g + 1) * NSA_HEAD_DIM)
        psum = jnp.zeros((tq, nc), F32)
        for r in range(NSA_GROUP):
            h = g * NSA_GROUP + r
            hl = slice(h * NSA_HEAD_DIM, (h + 1) * NSA_HEAD_DIM)
            p = _masked_softmax(_nt(q_ref[:, hl], kc[:, gl]), cmask)
            psum = psum + p
            c = GATE_LANE0 + 3 * h
            o_ref[:, hl] = _mm(p.astype(BF16), vc[:, gl]) * gate_ref[:, c:c + 1]
        score = jnp.where(forced, float(NSA_GROUP + 1), psum)
        score = jnp.where(future, -1.0, score)
        sel = jnp.logical_and(_top_blocks(score, blk, min(TOP_N, nc), nc), jnp.logical_not(future))
        selt_ref[0, g] = jnp.where(sel, 0.0, NEG).T


def _cmp_prompt(nq, kcc, vcc, misc, *, batch, seq, tq):
    nc = seq // CMP_BLOCK
    nt = seq // tq
    row = lambda n: pl.BlockSpec((tq, n), lambda b, i: (b * nt + i, 0))
    cblk = pl.BlockSpec((1, nc, KV_W), lambda b, i: (b, 0, 0))
    return pl.pallas_call(
        functools.partial(_cmp_kernel, tq=tq, nc=nc),
        grid=(batch, nt),
        in_specs=[row(NSA_WIDTH), cblk, cblk, row(LANE)],
        out_specs=[row(NSA_WIDTH),
                   pl.BlockSpec((1, NSA_KV_HEADS, nc, tq), lambda b, i: (b, 0, 0, i))],
        out_shape=[jax.ShapeDtypeStruct((batch * seq, NSA_WIDTH), F32),
                   jax.ShapeDtypeStruct((batch, NSA_KV_HEADS, nc, seq), F32)],
        compiler_params=pltpu.CompilerParams(dimension_semantics=("arbitrary", "arbitrary"),
                                             vmem_limit_bytes=VMEM_LIMIT),
        name="nsa_cmp_prompt",
    )(nq, kcc.reshape(batch, nc, KV_W), vcc.reshape(batch, nc, KV_W), misc)


def _slc_kernel(q_ref, ocmp_ref, gate_ref, selt_ref, e0_ref, ks_ref, vs_ref, kw_ref, vw_ref, o_ref,
                *, tq, tk):
    q0 = pl.program_id(1) * tq
    n_kt = (q0 + tq - 1) // tk + 1
    t = q0 + lax.broadcasted_iota(jnp.int32, (tq, 1), 0)
    bpt = tk // SEL_BLOCK
    wlen = WINDOW + tq
    w0 = pl.multiple_of(jnp.maximum(q0 - WINDOW, 0), tq)
    wpos = w0 + lax.broadcasted_iota(jnp.int32, (1, wlen), 1)
    wmask = (wpos <= t) & (t - wpos < WINDOW)
    rows = NSA_GROUP * tq
    for g in range(NSA_KV_HEADS):
        gl = slice(g * NSA_HEAD_DIM, (g + 1) * NSA_HEAD_DIM)
        q4 = jnp.concatenate(
            [q_ref[:, (g * NSA_GROUP + r) * NSA_HEAD_DIM:(g * NSA_GROUP + r + 1) * NSA_HEAD_DIM]
             for r in range(NSA_GROUP)], axis=0)

        def tile(j, carry):
            m, l, acc = carry
            k0 = pl.multiple_of(j * tk, tk)
            s = _nt(q4, ks_ref[0, pl.ds(k0, tk), gl])
            sb = selt_ref[0, g, pl.ds(pl.multiple_of(j * bpt, bpt), bpt), :]
            bias = _tn(sb.astype(BF16), e0_ref[...])
            kpos = k0 + lax.broadcasted_iota(jnp.int32, (1, tk), 1)
            bias = jnp.where(kpos <= t, bias, NEG)
            s = (s.reshape(NSA_GROUP, tq, tk) + bias[None]).reshape(rows, tk)
            m_new = jnp.maximum(m, jnp.max(s, axis=-1, keepdims=True))
            alpha = jnp.exp(m - m_new)
            p = jnp.exp(s - m_new)
            l = alpha * l + jnp.sum(p, axis=-1, keepdims=True)
            acc = alpha * acc + _mm(p.astype(BF16), vs_ref[0, pl.ds(k0, tk), gl])
            return m_new, l, acc

        init = (jnp.full((rows, 1), NEG, F32), jnp.zeros((rows, 1), F32),
                jnp.zeros((rows, NSA_HEAD_DIM), F32))
        _, l, acc = lax.fori_loop(0, n_kt, tile, init)
        o_slc = acc / l

        sw = _nt(q4, kw_ref[0, pl.ds(w0, wlen), gl]).reshape(NSA_GROUP, tq, wlen)
        pw = _masked_softmax(sw, wmask[None]).reshape(rows, wlen)
        o_win = _mm(pw.astype(BF16), vw_ref[0, pl.ds(w0, wlen), gl])

        for r in range(NSA_GROUP):
            h = g * NSA_GROUP + r
            hl = slice(h * NSA_HEAD_DIM, (h + 1) * NSA_HEAD_DIM)
            rs = slice(r * tq, (r + 1) * tq)
            c = GATE_LANE0 + 3 * h
            o_ref[:, hl] = (ocmp_ref[:, hl] + o_slc[rs] * gate_ref[:, c + 1:c + 2]
                            + o_win[rs] * gate_ref[:, c + 2:c + 3])


def _slc_prompt(nq, ocmp, misc, selt, ks, vs, kw, vw, *, batch, seq, tq, tk):
    nt = seq // tq
    nc = seq // SEL_BLOCK
    bpt = tk // SEL_BLOCK
    e0 = (jnp.arange(tk, dtype=jnp.int32)[None, :] // SEL_BLOCK
          == jnp.arange(bpt, dtype=jnp.int32)[:, None]).astype(BF16)
    row = lambda n: pl.BlockSpec((tq, n), lambda b, i: (b * nt + i, 0))
    kv = pl.BlockSpec((1, seq, KV_W), lambda b, i: (b, 0, 0), pipeline_mode=pl.Buffered(1))
    as_kv = lambda a: a.astype(BF16).reshape(batch, seq, KV_W)
    return pl.pallas_call(
        functools.partial(_slc_kernel, tq=tq, tk=tk),
        grid=(batch, nt),
        in_specs=[row(NSA_WIDTH), row(NSA_WIDTH), row(LANE),
                  pl.BlockSpec((1, NSA_KV_HEADS, nc, tq), lambda b, i: (b, 0, 0, i)),
                  _const_spec((bpt, tk)), kv, kv, kv, kv],
        out_specs=row(NSA_WIDTH),
        out_shape=jax.ShapeDtypeStruct((batch * seq, NSA_WIDTH), F32),
        compiler_params=pltpu.CompilerParams(dimension_semantics=("arbitrary", "arbitrary"),
                                             vmem_limit_bytes=VMEM_LIMIT),
        name="nsa_slc_prompt",
    )(nq, ocmp, misc, selt, e0, as_kv(ks), as_kv(vs), as_kv(kw), as_kv(vw))


def _pool_cmp_kernel(pk_ref, pv_ref, wk_ref, wv_ref, ok_ref, ov_ref, *, nb):
    for p_ref, w_ref, o_ref in ((pk_ref, wk_ref, ok_ref), (pv_ref, wv_ref, ov_ref)):
        o_ref[...] = jnp.sum(p_ref[...].reshape(nb, CMP_BLOCK, KV_W) * w_ref[...][None], axis=1)


def _pool_compress(pool_k, pool_v, wck, wcv, *, pp):
    n_blocks = pool_k.shape[0] // CMP_BLOCK
    nb = pp * (PAGE_SIZE // CMP_BLOCK)
    pg = pl.BlockSpec((pp * PAGE_SIZE, KV_W), lambda i: (i, 0))
    og = pl.BlockSpec((nb, KV_W), lambda i: (i, 0))
    return pl.pallas_call(
        functools.partial(_pool_cmp_kernel, nb=nb),
        grid=(n_blocks // nb,),
        in_specs=[pg, pg, _const_spec((CMP_BLOCK, KV_W)), _const_spec((CMP_BLOCK, KV_W))],
        out_specs=[og, og],
        out_shape=[jax.ShapeDtypeStruct((n_blocks, KV_W), F32)] * 2,
        compiler_params=pltpu.CompilerParams(dimension_semantics=("arbitrary",),
                                             vmem_limit_bytes=VMEM_LIMIT),
        name="pool_compress",
    )(pool_k, pool_v, wck, wcv)


def _cmp_step_kernel(pt_ref, q_ref, kcn_ref, vcn_ref, gate_ref, wk_ref, wv_ref, pkc_ref, pvc_ref,
                     o_ref, idx_ref, kbuf, vbuf, *, n_pages, ncp):
    b = pl.program_id(0)
    bpp = PAGE_SIZE // CMP_BLOCK
    n_past = n_pages * bpp
    t = n_pages * PAGE_SIZE

    def gather(j, carry):
        p = pt_ref[b, j]
        for src, dst in ((pkc_ref, kbuf), (pvc_ref, vbuf)):
            for u in range(bpp):
                dst[pl.ds(j * bpp + u, 1), :] = src[pl.ds(p * bpp + u, 1), :]
        return carry

    lax.fori_loop(0, n_pages, gather, 0)
    kbuf[n_past:ncp, :] = jnp.zeros((ncp - n_past, KV_W), F32)
    vbuf[n_past:ncp, :] = jnp.zeros((ncp - n_past, KV_W), F32)
    kbuf[n_past:n_past + 1, :] = kcn_ref[0] * wk_ref[0:1, :]
    vbuf[n_past:n_past + 1, :] = vcn_ref[0] * wv_ref[0:1, :]

    blk = lax.broadcasted_iota(jnp.int32, (1, ncp), 1)
    cmask = blk * CMP_BLOCK + (CMP_BLOCK - 1) <= t
    cur = t // SEL_BLOCK
    forced = (blk == 0) | (blk == cur) | (blk == cur - 1)
    kc = kbuf[...].astype(BF16)
    vc = vbuf[...].astype(BF16)
    lane = lax.broadcasted_iota(jnp.int32, (1, LANE), 1)
    idx_row = jnp.zeros((1, LANE), jnp.int32)
    qrow = q_ref[0].astype(F32)
    for g in range(NSA_KV_HEADS):
        gl = slice(g * NSA_HEAD_DIM, (g + 1) * NSA_HEAD_DIM)
        q4 = _group_rows(qrow, g)
        p = _masked_softmax(_nt(q4, kc[:, gl]), cmask)
        o4 = _mm(p.astype(BF16), vc[:, gl])
        for r in range(NSA_GROUP):
            h = g * NSA_GROUP + r
            c = GATE_LANE0 + 3 * h
            o_ref[0, :, h * NSA_HEAD_DIM:(h + 1) * NSA_HEAD_DIM] = o4[r:r + 1] * gate_ref[0, :, c:c + 1]
        score = jnp.where(forced, float(NSA_GROUP + 1),
                          jnp.sum(p[0:NSA_GROUP], axis=0, keepdims=True))
        score = jnp.where(blk > cur, -1.0, score)
        score = jnp.where(blk > n_past, -3.0, score)
        for n in range(TOP_N):
            m = jnp.max(score, axis=-1, keepdims=True)
            j = jnp.min(jnp.where(score == m, blk, ncp), axis=-1, keepdims=True)
            idx_row = jnp.where(lane == g * TOP_N + n, j, idx_row)
            score = jnp.where(blk == j, -4.0, score)
    idx_ref[0] = idx_row


def _cmp_step(page_table, nq, kcn, vcn, misc, wck, wcv, pkc, pvc):
    n, n_pages = page_table.shape
    bpp = PAGE_SIZE // CMP_BLOCK
    ncp = -(-(n_pages * bpp + 1) // LANE) * LANE
    r3 = lambda a: a.reshape(n, 1, a.shape[-1])
    one = lambda w: pl.BlockSpec((1, 1, w), lambda i, pt: (i, 0, 0))
    cst = lambda shape: pl.BlockSpec(shape, lambda i, pt: (0, 0), pipeline_mode=pl.Buffered(1))
    return pl.pallas_call(
        functools.partial(_cmp_step_kernel, n_pages=n_pages, ncp=ncp),
        grid_spec=pltpu.PrefetchScalarGridSpec(
            num_scalar_prefetch=1, grid=(n,),
            in_specs=[one(NSA_WIDTH), one(KV_W), one(KV_W), one(LANE),
                      cst((CMP_BLOCK, KV_W)), cst((CMP_BLOCK, KV_W)),
                      cst(pkc.shape), cst(pvc.shape)],
            out_specs=[one(NSA_WIDTH), one(LANE)],
            scratch_shapes=[pltpu.VMEM((ncp, KV_W), F32), pltpu.VMEM((ncp, KV_W), F32)]),
        out_shape=[jax.ShapeDtypeStruct((n, 1, NSA_WIDTH), F32),
                   jax.ShapeDtypeStruct((n, 1, LANE), jnp.int32)],
        compiler_params=pltpu.CompilerParams(dimension_semantics=("arbitrary",),
                                             vmem_limit_bytes=VMEM_LIMIT),
        name="nsa_cmp_step",
    )(page_table, r3(nq), r3(kcn), r3(vcn), r3(misc), wck, wcv, pkc, pvc)


def _slc_step_kernel(pt_ref, idx_ref, q_ref, ocmp_ref, gate_ref, ksn_ref, vsn_ref, kwn_ref, vwn_ref,
                     kwb_ref, vwb_ref, pks_ref, pvs_ref, o_ref, kbuf, vbuf, sem, *, n_pages, wb):
    b = pl.program_id(0)
    bpp = PAGE_SIZE // SEL_BLOCK
    n_past = n_pages * bpp
    t = n_pages * PAGE_SIZE
    n_sel = NSA_KV_HEADS * TOP_N

    def copies(i):
        blk = jnp.minimum(idx_ref[b, i], n_past - 1)
        phys = pt_ref[b, blk // bpp] * bpp + blk % bpp
        return (pltpu.make_async_copy(pks_ref.at[phys], kbuf.at[i], sem.at[0, i]),
                pltpu.make_async_copy(pvs_ref.at[phys], vbuf.at[i], sem.at[1, i]))

    for i in range(n_sel):
        for cp in copies(i):
            cp.start()
    for i in range(n_sel):
        for cp in copies(i):
            cp.wait()

    row0 = lax.broadcasted_iota(jnp.int32, (SEL_BLOCK, 1), 0) == 0
    kidx = lax.broadcasted_iota(jnp.int32, (1, TOP_N * SEL_BLOCK), 1)
    wpos = t - wb + lax.broadcasted_iota(jnp.int32, (1, wb), 1)
    wmask = (wpos >= 0) & (wpos <= t) & (t - wpos < WINDOW)
    qrow = q_ref[0].astype(F32)
    for g in range(NSA_KV_HEADS):
        gl = slice(g * NSA_HEAD_DIM, (g + 1) * NSA_HEAD_DIM)
        q4 = _group_rows(qrow, g)
        k_new = jnp.where(row0, ksn_ref[0, :, gl], 0.0)
        v_new = jnp.where(row0, vsn_ref[0, :, gl], 0.0)
        k_parts, v_parts = [], []
        kblk = jnp.zeros((1, TOP_N * SEL_BLOCK), jnp.int32)
        for n in range(TOP_N):
            i = g * TOP_N + n
            blk = idx_ref[b, i]
            is_new = blk >= n_past
            k_parts.append(jnp.where(is_new, k_new, kbuf[i, :, gl]).astype(BF16))
            v_parts.append(jnp.where(is_new, v_new, vbuf[i, :, gl]).astype(BF16))
            kblk = jnp.where(kidx // SEL_BLOCK == n, blk, kblk)
        valid = kblk * SEL_BLOCK + kidx % SEL_BLOCK <= t
        p = _masked_softmax(_nt(q4, jnp.concatenate(k_parts, axis=0)), valid)
        o_slc = _mm(p.astype(BF16), jnp.concatenate(v_parts, axis=0))

        sb = jnp.where(wmask, _nt(q4, kwb_ref[0, :, gl].astype(BF16)), NEG)
        kn = kwn_ref[0, :, gl].astype(BF16).astype(F32)
        sn = jnp.sum(q4.astype(F32) * kn, axis=-1, keepdims=True)
        m = jnp.maximum(jnp.max(sb, axis=-1, keepdims=True), sn)
        eb = jnp.where(wmask, jnp.exp(sb - m), 0.0)
        en = jnp.exp(sn - m)
        den = jnp.sum(eb, axis=-1, keepdims=True) + en
        vn = vwn_ref[0, :, gl].astype(BF16).astype(F32)
        o_win = (_mm((eb / den).astype(BF16), vwb_ref[0, :, gl].astype(BF16))
                 + (en / den).astype(BF16).astype(F32) * vn)
        for r in range(NSA_GROUP):
            h = g * NSA_GROUP + r
            hl = slice(h * NSA_HEAD_DIM, (h + 1) * NSA_HEAD_DIM)
            c = GATE_LANE0 + 3 * h
            o_ref[0, :, hl] = (ocmp_ref[0, :, hl] + o_slc[r:r + 1] * gate_ref[0, :, c + 1:c + 2]
                               + o_win[r:r + 1] * gate_ref[0, :, c + 2:c + 3])


def _slc_step(page_table, idx, nq, ocmp, misc, ksn, vsn, kwn, vwn, buf_kw, buf_vw, pool_ks, pool_vs):
    n, n_pages = page_table.shape
    wb = buf_kw.shape[1]
    n_sel = NSA_KV_HEADS * TOP_N
    r3 = lambda a: a.reshape(n, 1, a.shape[-1])
    one = lambda w: pl.BlockSpec((1, 1, w), lambda i, pt, ix: (i, 0, 0))
    win = pl.BlockSpec((1, wb, KV_W), lambda i, pt, ix: (i, 0, 0))
    hbm = pl.BlockSpec(memory_space=pl.ANY)
    blocks = lambda pool: pool.reshape(-1, SEL_BLOCK, KV_W)
    return pl.pallas_call(
        functools.partial(_slc_step_kernel, n_pages=n_pages, wb=wb),
        grid_spec=pltpu.PrefetchScalarGridSpec(
            num_scalar_prefetch=2, grid=(n,),
            in_specs=[one(NSA_WIDTH), one(NSA_WIDTH), one(LANE), one(KV_W), one(KV_W), one(KV_W),
                      one(KV_W), win, win, hbm, hbm],
            out_specs=one(NSA_WIDTH),
            scratch_shapes=[pltpu.VMEM((n_sel, SEL_BLOCK, KV_W), F32),
                            pltpu.VMEM((n_sel, SEL_BLOCK, KV_W), F32),
                            pltpu.SemaphoreType.DMA((2, n_sel))]),
        out_shape=jax.ShapeDtypeStruct((n, 1, NSA_WIDTH), F32),
        compiler_params=pltpu.CompilerParams(dimension_semantics=("arbitrary",),
                                             vmem_limit_bytes=VMEM_LIMIT),
        name="nsa_slc_step",
    )(page_table, idx, r3(nq), ocmp, r3(misc), r3(ksn), r3(vsn), r3(kwn), r3(vwn),
      buf_kw.reshape(n, wb, KV_W), buf_vw.reshape(n, wb, KV_W), blocks(pool_ks), blocks(pool_vs))


def _post_kernel(x_ref, og_ref, on_ref, pe_ref, p0_ref, p1_ref, wo_ref, gf_ref, wup_ref, wcv_ref,
                 bcv_ref, wdn_ref, gp_ref, wple_ref, wpg_ref, gfin_ref, y_ref, ut_ref,
                 carry, ext, *, tm, tail, shift):
    attn = (_mm(og_ref[...].astype(BF16), wo_ref[0:GLA_WIDTH, :])
            + _mm(on_ref[...].astype(BF16), wo_ref[GLA_WIDTH:, :]))
    h = x_ref[...] + attn
    hn = _rms(h, gf_ref[...]).astype(BF16)
    if shift:
        @pl.when(pl.program_id(1) == 0)
        def _():
            carry[...] = jnp.zeros_like(carry)

    def chunk(j, acc):
        u = _mm(hn, wup_ref[j])
        wc = wcv_ref[j]
        if shift:
            ext[0:SUBLANE, :] = carry[j]
            ext[SUBLANE:, :] = u
            um2 = ext[pl.ds(SUBLANE - 2, tm), :]
            um1 = ext[pl.ds(SUBLANE - 1, tm), :]
            carry[j] = u[tm - SUBLANE:, :]
        else:
            um2 = p0_ref[j]
            um1 = p1_ref[j]
        c = bcv_ref[j] + um2 * wc[0:1] + um1 * wc[1:2] + u * wc[2:3]
        ut_ref[0, j] = u[tm - tail:, :]
        a = c[:, :FF_CHUNK]
        act = (a * _sigmoid(a) * c[:, FF_CHUNK:]).astype(BF16)
        return acc + _mm(act, wdn_ref[j])

    h = h + lax.fori_loop(0, N_FF_CHUNKS, chunk, jnp.zeros((tm, D_MODEL), F32))
    gate = _sigmoid(_mm(_rms(h, gp_ref[...]).astype(BF16), wpg_ref[...]))
    h = h + _mm(pe_ref[...].astype(BF16), wple_ref[...]) * gate
    y_ref[...] = _rms(h, gfin_ref[...])


def _post(x2d, ogla, onsa, pe2d, prev, pw, g_final, *, batch, tm, shift):
    m = x2d.shape[0]
    nt = m // (batch * tm)
    tail = SUBLANE if shift else tm
    row = lambda n: pl.BlockSpec((tm, n), lambda b, i: (b * nt + i, 0))
    cst = lambda shape: pl.BlockSpec(shape, lambda b, i: (0,) * len(shape),
                                     pipeline_mode=pl.Buffered(1))
    if shift:
        prev_spec = cst((1, 1, LANE))
        p0 = p1 = jnp.zeros((1, 1, LANE), F32)
    else:
        prev_spec = pl.BlockSpec((N_FF_CHUNKS, tm, 2 * FF_CHUNK), lambda b, i: (0, b * nt + i, 0))
        p0, p1 = prev
    return pl.pallas_call(
        functools.partial(_post_kernel, tm=tm, tail=tail, shift=shift),
        grid=(batch, nt),
        in_specs=[row(D_MODEL), row(GLA_WIDTH), row(NSA_WIDTH), row(PLE_DIM), prev_spec, prev_spec,
                  cst((D_MODEL, D_MODEL)), cst((1, D_MODEL)),
                  cst((N_FF_CHUNKS, D_MODEL, 2 * FF_CHUNK)), cst((N_FF_CHUNKS, CONV_W, 2 * FF_CHUNK)),
                  cst((N_FF_CHUNKS, 1, 2 * FF_CHUNK)), cst((N_FF_CHUNKS, FF_CHUNK, D_MODEL)),
                  cst((1, D_MODEL)), cst((PLE_DIM, D_MODEL)), cst((D_MODEL, D_MODEL)),
                  cst((1, D_MODEL))],
        out_specs=[row(D_MODEL),
                   pl.BlockSpec((1, N_FF_CHUNKS, tail, 2 * FF_CHUNK), lambda b, i: (b, 0, 0, 0))],
        out_shape=[jax.ShapeDtypeStruct((m, D_MODEL), F32),
                   jax.ShapeDtypeStruct((batch, N_FF_CHUNKS, tail, 2 * FF_CHUNK), F32)],
        scratch_shapes=[pltpu.VMEM((N_FF_CHUNKS, SUBLANE, 2 * FF_CHUNK), F32),
                        pltpu.VMEM((tm + SUBLANE, 2 * FF_CHUNK), F32)],
        compiler_params=pltpu.CompilerParams(dimension_semantics=("arbitrary", "arbitrary"),
                                             vmem_limit_bytes=VMEM_LIMIT),
        name="post",
    )(x2d, ogla, onsa, pe2d, p0, p1, pw["w_o"], pw["g_ffn"], pw["w_up"], pw["w_conv"], pw["b_conv"],
      pw["w_down"], pw["g_ple"], pw["w_ple"], pw["w_ple_gate"], g_final)


def _ff_chunked(a):
    lead = a.shape[:-1]
    a = a.reshape(*lead, 2, N_FF_CHUNKS, FF_CHUNK)
    a = jnp.moveaxis(a, -2, 0)
    return a.reshape(N_FF_CHUNKS, *lead, 2 * FF_CHUNK)


def _ff_unchunked(a):
    r = a.shape[1]
    return a.reshape(N_FF_CHUNKS, r, 2, FF_CHUNK).transpose(1, 2, 0, 3).reshape(r, 2 * D_FF)


def _pack_weights(i, g_attn, w_in, w_gla_gate, b_gla_gate, g_gla_out, b_nsa_gate, w_cmp_k, w_cmp_v,
                  w_o, g_ffn, w_up, w_conv, b_conv, w_down, g_ple, w_ple, w_ple_gate):
    w = w_in[i]
    o_glr = 2 * GLA_QK + 2 * GLA_WIDTH
    o_nq = o_glr + GLA_GATE_RANK
    o_ng = o_nq + NSA_WIDTH + 6 * KV_W
    pad = jnp.zeros((D_MODEL, LANE - GLA_GATE_RANK - 3 * NSA_HEADS), w.dtype)
    w_pack = jnp.concatenate([w[:, :o_glr], w[:, o_nq:o_ng], w[:, o_glr:o_nq], w[:, o_ng:], pad], axis=1)
    w_gate = jnp.zeros((LANE, GLA_QK), F32).at[:GLA_GATE_RANK].set(w_gla_gate[i])
    b_misc = jnp.zeros((1, LANE), F32).at[0, GATE_LANE0:GATE_LANE0 + 3 * NSA_HEADS].set(b_nsa_gate[i])
    return {
        "g_attn": g_attn[i][None], "w_in": w_pack.astype(BF16), "w_gate": w_gate.astype(BF16),
        "b_gate": b_gla_gate[i][None], "b_misc": b_misc, "g_gla_out": g_gla_out[i][None],
        "w_ck": jnp.repeat(w_cmp_k[i], NSA_HEAD_DIM, axis=1),
        "w_cv": jnp.repeat(w_cmp_v[i], NSA_HEAD_DIM, axis=1),
        "w_o": w_o[i].astype(BF16), "g_ffn": g_ffn[i][None],
        "w_up": _ff_chunked(w_up[i]).astype(BF16), "w_conv": _ff_chunked(w_conv[i]),
        "b_conv": _ff_chunked(b_conv[i][None]),
        "w_down": w_down[i].reshape(N_FF_CHUNKS, FF_CHUNK, D_MODEL).astype(BF16),
        "g_ple": g_ple[i][None], "w_ple": w_ple[i].astype(BF16),
        "w_ple_gate": w_ple_gate[i].astype(BF16),
    }


def _rope_tables(pos):
    half = ROT_DIM // 2
    inv = ROPE_THETA ** (-jnp.arange(half, dtype=F32) / half)
    ang = pos.astype(F32)[:, None] * inv[None, :]
    d = jnp.arange(LANE) % NSA_HEAD_DIM
    cos = jnp.cos(ang)[:, d % half]
    sin = jnp.sin(ang)[:, d % half]
    return (jnp.where(d < ROT_DIM, cos, 1.0), jnp.where(d < half, -sin, 0.0),
            jnp.where((d >= half) & (d < ROT_DIM), sin, 0.0))


def _kv4(a, b, t):
    return a.reshape(b, t, NSA_KV_HEADS, NSA_HEAD_DIM)


def _prompt_layer(x, pe, pw, g_final):
    b, t, _ = x.shape
    m = b * t
    tabs = _rope_tables(jnp.arange(t, dtype=jnp.int32))
    (gqk, gv, gr, la, nq, kc, vc, ks, vs, kw, vw, misc, kcc, vcc) = _inproj(
        x.reshape(m, D_MODEL), tabs, pw, tm=512, compress=True)
    ogla, h_gla = _gla_prompt(gqk, gv, gr, la, pw["g_gla_out"], batch=b, seq=t, tg=512)
    ocmp, selt = _cmp_prompt(nq, kcc, vcc, misc, batch=b, seq=t, tq=128)
    onsa = _slc_prompt(nq, ocmp, misc, selt, ks, vs, kw, vw, batch=b, seq=t, tq=128, tk=512)
    y, utail = _post(x.reshape(m, D_MODEL), ogla, onsa, pe.reshape(m, PLE_DIM), None, pw, g_final,
                     batch=b, tm=512, shift=True)
    conv_new = jax.vmap(_ff_unchunked)(utail)[:, SUBLANE - (CONV_W - 1):]
    nw = min(WINDOW, t)
    state = (_kv4(kc, b, t), _kv4(vc, b, t), _kv4(ks, b, t), _kv4(vs, b, t),
             _kv4(kw, b, t)[:, t - nw:], _kv4(vw, b, t)[:, t - nw:], h_gla, conv_new)
    return y.reshape(b, t, D_MODEL), state


def _sample_layer(x, pe, pw, g_final, pools, bufs, h0, conv_prev, page_table):
    n, t, _ = x.shape
    assert t == 1, "the sample group is written for one new token per request"
    n_pages = page_table.shape[1]
    pos = jnp.full((n,), n_pages * PAGE_SIZE, jnp.int32)
    tabs = _rope_tables(pos)
    x2d = x.reshape(n, D_MODEL)
    (gqk, gv, gr, la, nq, kc, vc, ks, vs, kw, vw, misc) = _inproj(x2d, tabs, pw, tm=n, compress=False)
    ogla, h_gla = _gla_step(gqk, gv, gr, la, pw["g_gla_out"], h0, bb=SUBLANE)
    pool_kc, pool_vc, pool_ks, pool_vs = pools
    n_pool = pool_kc.shape[0]
    pp = max(p for p in range(4, 65, 4) if n_pool % p == 0)
    pkc, pvc = _pool_compress(pool_kc.reshape(-1, KV_W), pool_vc.reshape(-1, KV_W),
                              pw["w_ck"], pw["w_cv"], pp=pp)
    ocmp, idx = _cmp_step(page_table, nq, kc, vc, misc, pw["w_ck"], pw["w_cv"], pkc, pvc)
    buf_kw, buf_vw = bufs
    onsa = _slc_step(page_table, idx.reshape(n, LANE)[:, :NSA_KV_HEADS * TOP_N], nq, ocmp, misc,
                     ks, vs, kw, vw, buf_kw, buf_vw, pool_ks, pool_vs)
    prev = (_ff_chunked(conv_prev[:, 0]), _ff_chunked(conv_prev[:, 1]))
    y, utail = _post(x2d, ogla, onsa.reshape(n, NSA_WIDTH), pe.reshape(n, PLE_DIM), prev, pw, g_final,
                     batch=1, tm=n, shift=False)
    u = _ff_unchunked(utail[0])
    conv_new = jnp.stack([conv_prev[:, 1], u], axis=1)
    wb = buf_kw.shape[1]
    nw = min(WINDOW, wb + 1)
    kwin = jnp.concatenate([buf_kw, _kv4(kw, n, 1)], axis=1)[:, wb + 1 - nw:]
    vwin = jnp.concatenate([buf_vw, _kv4(vw, n, 1)], axis=1)[:, wb + 1 - nw:]
    state = (_kv4(kc, n, 1), _kv4(vc, n, 1), _kv4(ks, n, 1), _kv4(vs, n, 1), kwin, vwin, h_gla,
             conv_new)
    return y.reshape(n, 1, D_MODEL), state


def kernel(x_prompt, x_sample, p_prompt, p_sample, cache_k_cmp, cache_v_cmp, cache_k_slc, cache_v_slc, cache_k_win, cache_v_win, state_gla, state_conv, page_table, g_attn, w_in, w_gla_gate, b_gla_gate, g_gla_out, b_nsa_gate, w_cmp_k, w_cmp_v, w_o, g_ffn, w_up, w_conv, b_conv, w_down, g_ple, w_ple, w_ple_gate, g_final):
    depth = w_in.shape[0]
    assert depth == 1, "the final norm is fused into the layer kernel; written for a one-layer trunk"
    pw = _pack_weights(0, g_attn, w_in, w_gla_gate, b_gla_gate, g_gla_out, b_nsa_gate, w_cmp_k,
                       w_cmp_v, w_o, g_ffn, w_up, w_conv, b_conv, w_down, g_ple, w_ple, w_ple_gate)
    gfin = g_final[None]
    y_p, st_p = _prompt_layer(x_prompt, p_prompt[0], pw, gfin)
    y_s, st_s = _sample_layer(
        x_sample, p_sample[0], pw, gfin,
        (cache_k_cmp[0], cache_v_cmp[0], cache_k_slc[0], cache_v_slc[0]),
        (cache_k_win[0], cache_v_win[0]), state_gla[0], state_conv[0], page_table)
    return (y_p, y_s, *[a[None] for a in st_p], *[a[None] for a in st_s])
```

```python
import functools

import jax
import jax.numpy as jnp
from jax import lax
from jax.experimental import pallas as pl
from jax.experimental.pallas import tpu as pltpu

F32 = jnp.float32
BF16 = jnp.bfloat16

D_MODEL = 1024
PAGE_SIZE = 128
GLA_WIDTH = 512
GLA_HEADS = 4
GLA_QK = 256
GLA_DK = 64
GLA_DV = 128
GLA_GATE_RANK = 16
GLA_TAU = 16.0
GLA_CHUNK = 64
NSA_WIDTH = 512
NSA_HEADS = 8
NSA_HEAD_DIM = 64
NSA_KV_HEADS = 2
NSA_GROUP = 4
KV_W = 128
CMP_BLOCK = 64
SEL_BLOCK = 64
TOP_N = 16
WINDOW = 512
ROT_DIM = 16
ROPE_THETA = 500000.0
D_FF = 2816
CONV_W = 3
PLE_DIM = 256
EPS = 1e-6

LANE = 128
SUBLANE = 8
NEG = -1e30
FF_CHUNK = 256
N_FF_CHUNKS = D_FF // FF_CHUNK
VMEM_LIMIT = 56 * 1024 * 1024

C_GQ, C_GK, C_GV, C_GR, C_NQ = 0, 256, 512, 1024, 1536
C_KC, C_VC, C_KS, C_VS, C_KW, C_VW, C_MISC = 2048, 2176, 2304, 2432, 2560, 2688, 2816
D_PACK = 2944
GATE_LANE0 = GLA_GATE_RANK


def _nt(a, b):
    return lax.dot_general(a, b, (((1,), (1,)), ((), ())), preferred_element_type=F32)


def _tn(a, b):
    return lax.dot_general(a, b, (((0,), (0,)), ((), ())), preferred_element_type=F32)


def _mm(a, b):
    return jnp.dot(a, b, preferred_element_type=F32)


def _rms(x, g):
    return x * lax.rsqrt(jnp.mean(x * x, axis=-1, keepdims=True) + EPS) * g


def _sigmoid(x):
    return 1.0 / (1.0 + jnp.exp(-x))


def _group_rows(qrow, g):
    parts = [qrow[:, (g * NSA_GROUP + r) * NSA_HEAD_DIM:(g * NSA_GROUP + r + 1) * NSA_HEAD_DIM]
             for r in range(NSA_GROUP)]
    parts.append(jnp.zeros((SUBLANE - NSA_GROUP, NSA_HEAD_DIM), F32))
    return jnp.concatenate(parts, axis=0).astype(BF16)


def _const_spec(shape):
    nd = len(shape)
    return pl.BlockSpec(shape, lambda *_: (0,) * nd, pipeline_mode=pl.Buffered(1))


def _inproj_kernel(x_ref, g_ref, w_ref, wgate_ref, bgate_ref, bmisc_ref, cos_ref, sa_ref, sb_ref,
                   wck_ref, wcv_ref, *out_refs, prompt):
    gqk_ref, gv_ref, gr_ref, la_ref, nq_ref, misc_ref = out_refs[:6]
    kv_refs = out_refs[6:12]
    xn = _rms(x_ref[...], g_ref[...]).astype(BF16)

    def proj(lo, hi):
        return _mm(xn, w_ref[:, lo:hi])

    cos, sa, sb = cos_ref[...], sa_ref[...], sb_ref[...]

    def rope(z):
        return z * cos + pltpu.roll(z, LANE - ROT_DIM // 2, 1) * sa + pltpu.roll(z, ROT_DIM // 2, 1) * sb

    gqk_ref[...] = proj(C_GQ, C_GV)
    gv_ref[...] = proj(C_GV, C_GR)
    gr_ref[...] = proj(C_GR, C_NQ)
    for j in range(NSA_WIDTH // LANE):
        z = proj(C_NQ + j * LANE, C_NQ + (j + 1) * LANE)
        nq_ref[:, j * LANE:(j + 1) * LANE] = (rope(z) * (NSA_HEAD_DIM ** -0.5)).astype(BF16)
    kv = [rope(proj(C_KC, C_VC)), proj(C_VC, C_KS), rope(proj(C_KS, C_VS)), proj(C_VS, C_KW),
          rope(proj(C_KW, C_VW)), proj(C_VW, C_MISC)]
    z = proj(C_MISC, D_PACK)
    misc_ref[...] = _sigmoid(z + bmisc_ref[...])
    pre = _mm(z.astype(BF16), wgate_ref[...]) + bgate_ref[...]
    la_ref[...] = (jnp.minimum(pre, 0.0) - jnp.log(1.0 + jnp.exp(-jnp.abs(pre)))) * (1.0 / GLA_TAU)
    if not prompt:
        for ref, a in zip(kv_refs, kv):
            ref[...] = a
        return
    kv_t = [a.T for a in kv]
    for ref, a in zip(kv_refs, kv_t):
        ref[0] = a
    ks16_ref, vs16_ref, kw16_ref, vw16_ref, kcc_ref, vcc_ref = out_refs[12:]
    ks16_ref[...] = kv[2].astype(BF16)
    kw16_ref[...] = kv[4].astype(BF16)
    vs16_ref[0] = kv_t[3].astype(BF16)
    vw16_ref[0] = kv_t[5].astype(BF16)
    nb = kv[0].shape[0] // CMP_BLOCK
    kcc_ref[...] = jnp.sum(kv[0].reshape(nb, CMP_BLOCK, KV_W) * wck_ref[...][None], axis=1)
    vcc_ref[...] = jnp.sum(kv[1].reshape(nb, CMP_BLOCK, KV_W) * wcv_ref[...][None], axis=1)


def _inproj(x2d, tabs, pw, *, tm, batch=None):
    m = x2d.shape[0]
    prompt = batch is not None
    n_tab = tabs[0].shape[0] // tm
    row = lambda n: pl.BlockSpec((tm, n), lambda i: (i, 0))
    tab = pl.BlockSpec((tm, LANE), lambda i: (i % n_tab, 0))
    widths = [2 * GLA_QK, GLA_WIDTH, GLA_WIDTH, GLA_QK, NSA_WIDTH, LANE]
    dtypes = [F32, F32, F32, F32, BF16, F32]
    out_shape = [jax.ShapeDtypeStruct((m, n), d) for n, d in zip(widths, dtypes)]
    out_specs = [row(n) for n in widths]
    if prompt:
        seq = m // batch
        nt = seq // tm
        kvt = pl.BlockSpec((1, KV_W, tm), lambda i: (i // nt, 0, i % nt))
        nb = tm // CMP_BLOCK
        rows16 = jax.ShapeDtypeStruct((m, KV_W), BF16)
        cols16 = jax.ShapeDtypeStruct((batch, KV_W, seq), BF16)
        out_shape += ([jax.ShapeDtypeStruct((batch, KV_W, seq), F32)] * 6
                      + [rows16, cols16, rows16, cols16]
                      + [jax.ShapeDtypeStruct((m // CMP_BLOCK, KV_W), F32)] * 2)
        out_specs += ([kvt] * 6 + [row(KV_W), kvt, row(KV_W), kvt]
                      + [pl.BlockSpec((nb, KV_W), lambda i: (i, 0))] * 2)
    else:
        out_shape += [jax.ShapeDtypeStruct((m, KV_W), F32)] * 6
        out_specs += [row(KV_W)] * 6
    return pl.pallas_call(
        functools.partial(_inproj_kernel, prompt=prompt),
        grid=(m // tm,),
        in_specs=[row(D_MODEL), _const_spec((1, D_MODEL)), _const_spec((D_MODEL, D_PACK)),
                  _const_spec((LANE, GLA_QK)), _const_spec((1, GLA_QK)), _const_spec((1, LANE)),
                  tab, tab, tab, _const_spec((CMP_BLOCK, KV_W)), _const_spec((CMP_BLOCK, KV_W))],
        out_specs=out_specs, out_shape=out_shape,
        compiler_params=pltpu.CompilerParams(dimension_semantics=("arbitrary",),
                                             vmem_limit_bytes=VMEM_LIMIT),
        name="inproj",
    )(x2d, pw["g_attn"], pw["w_in"], pw["w_gate"], pw["b_gate"], pw["b_misc"], *tabs,
      pw["w_ck"], pw["w_cv"])


def _gla_kernel(qk_ref, v_ref, r_ref, la_ref, g_ref, o_ref, hout_ref, ht_scr, o_scr, *, n_chunks):
    t = pl.program_id(1)

    @pl.when(t == 0)
    def _():
        ht_scr[...] = jnp.zeros_like(ht_scr)

    ri = lax.broadcasted_iota(jnp.int32, (GLA_CHUNK, GLA_CHUNK), 0)
    ci = lax.broadcasted_iota(jnp.int32, (GLA_CHUNK, GLA_CHUNK), 1)
    causal = ri >= ci
    tril = causal.astype(F32)

    def chunk(c, carry):
        r0 = pl.multiple_of(c * GLA_CHUNK, GLA_CHUNK)
        rows = pl.ds(r0, GLA_CHUNK)
        bc = jnp.dot(tril, la_ref[rows, :], preferred_element_type=F32,
                     precision=lax.Precision.HIGHEST)
        blast = bc[GLA_CHUNK - 1:GLA_CHUNK, :]
        q = qk_ref[rows, 0:GLA_QK]
        k = qk_ref[rows, GLA_QK:2 * GLA_QK]
        qe = (q * jnp.exp(bc) * (GLA_DK ** -0.5)).astype(BF16)
        ke = (k * jnp.exp(-bc)).astype(BF16)
        kd = (k * jnp.exp(blast - bc)).astype(BF16)
        decay = jnp.exp(blast)
        for h in range(GLA_HEADS):
            ks = slice(h * GLA_DK, (h + 1) * GLA_DK)
            vh = v_ref[rows, h * GLA_DV:(h + 1) * GLA_DV].astype(BF16)
            ht = ht_scr[h]
            att = jnp.where(causal, _nt(qe[:, ks], ke[:, ks]), 0.0).astype(BF16)
            o_scr[rows, h * GLA_DV:(h + 1) * GLA_DV] = _mm(att, vh) + _nt(qe[:, ks], ht.astype(BF16))
            ht_scr[h] = ht * decay[:, ks] + _tn(vh, kd[:, ks])
        return carry

    lax.fori_loop(0, n_chunks, chunk, 0)
    for h in range(GLA_HEADS):
        hs = slice(h * GLA_DV, (h + 1) * GLA_DV)
        r = r_ref[:, hs]
        o_ref[:, hs] = _rms(o_scr[:, hs], g_ref[:, hs]) * (r * _sigmoid(r))

    @pl.when(t == pl.num_programs(1) - 1)
    def _():
        for h in range(GLA_HEADS):
            hout_ref[0, h] = ht_scr[h].T


def _gla_prompt(gqk, gv, gr, la, g_out, *, batch, seq, tg):
    nt = seq // tg
    row = lambda n: pl.BlockSpec((tg, n), lambda b, t: (b * nt + t, 0))
    return pl.pallas_call(
        functools.partial(_gla_kernel, n_chunks=tg // GLA_CHUNK),
        grid=(batch, nt),
        in_specs=[row(2 * GLA_QK), row(GLA_WIDTH), row(GLA_WIDTH), row(GLA_QK),
                  pl.BlockSpec((1, GLA_WIDTH), lambda b, t: (0, 0))],
        out_specs=[row(GLA_WIDTH),
                   pl.BlockSpec((1, GLA_HEADS, GLA_DK, GLA_DV), lambda b, t: (b, 0, 0, 0))],
        out_shape=[jax.ShapeDtypeStruct((batch * seq, GLA_WIDTH), F32),
                   jax.ShapeDtypeStruct((batch, GLA_HEADS, GLA_DK, GLA_DV), F32)],
        scratch_shapes=[pltpu.VMEM((GLA_HEADS, GLA_DV, GLA_DK), F32),
                        pltpu.VMEM((tg, GLA_WIDTH), F32)],
        compiler_params=pltpu.CompilerParams(dimension_semantics=("arbitrary", "arbitrary"),
                                             vmem_limit_bytes=VMEM_LIMIT),
        name="gla_prompt",
    )(gqk, gv, gr, la, g_out)


def _gla_step_kernel(qt_ref, kt_ref, at_ref, v_ref, r_ref, g_ref, h0_ref, o_ref, h_ref, *, bb):
    for b in range(bb):
        for h in range(GLA_HEADS):
            ds_ = slice(h * GLA_DK, (h + 1) * GLA_DK)
            vs_ = slice(h * GLA_DV, (h + 1) * GLA_DV)
            qcol = qt_ref[0, ds_, b:b + 1] * (GLA_DK ** -0.5)
            kcol = kt_ref[0, ds_, b:b + 1]
            acol = jnp.exp(at_ref[0, ds_, b:b + 1])
            hn = acol * h0_ref[b, h] + kcol * v_ref[b:b + 1, vs_]
            h_ref[b, h] = hn
            o = jnp.sum(qcol * hn, axis=0, keepdims=True)
            r = r_ref[b:b + 1, vs_]
            o_ref[b:b + 1, vs_] = _rms(o, g_ref[:, vs_]) * (r * _sigmoid(r))


def _gla_step(gqk, gv, gr, la, g_out, h0, *, bb):
    n = gqk.shape[0]
    ns = n // bb

    def cols(a):
        return a.reshape(ns, bb, GLA_QK).transpose(0, 2, 1)

    colspec = pl.BlockSpec((1, GLA_QK, bb), lambda i: (i, 0, 0))
    row = pl.BlockSpec((bb, GLA_WIDTH), lambda i: (i, 0))
    st = pl.BlockSpec((bb, GLA_HEADS, GLA_DK, GLA_DV), lambda i: (i, 0, 0, 0))
    return pl.pallas_call(
        functools.partial(_gla_step_kernel, bb=bb),
        grid=(ns,),
        in_specs=[colspec, colspec, colspec, row, row,
                  pl.BlockSpec((1, GLA_WIDTH), lambda i: (0, 0)), st],
        out_specs=[row, st],
        out_shape=[jax.ShapeDtypeStruct((n, GLA_WIDTH), F32),
                   jax.ShapeDtypeStruct(h0.shape, F32)],
        compiler_params=pltpu.CompilerParams(dimension_semantics=("arbitrary",)),
        name="gla_step",
    )(cols(gqk[:, :GLA_QK]), cols(gqk[:, GLA_QK:]), cols(la), gv, gr, g_out, h0)


def _masked_softmax(s, mask):
    s = jnp.where(mask, s, NEG)
    m = jnp.max(s, axis=-1, keepdims=True)
    e = jnp.where(mask, jnp.exp(s - m), 0.0)
    return e / jnp.maximum(jnp.sum(e, axis=-1, keepdims=True), 1e-30)


def _first_max(score, blk):
    m = jnp.max(score, axis=-1, keepdims=True)
    return jnp.min(jnp.where(score == m, blk, score.shape[-1]), axis=-1, keepdims=True)


def _top_blocks(score, blk, n_pick):
    sel = jnp.zeros(score.shape, jnp.bool_)
    for _ in range(n_pick):
        pick = blk == _first_max(score, blk)
        sel = jnp.logical_or(sel, pick)
        score = jnp.where(pick, -2.0, score)
    return sel


def _cmp_kernel(q_ref, kc_ref, vc_ref, gate_ref, o_ref, selt_ref, *, tq, nc):
    q0 = pl.program_id(1) * tq
    t = q0 + lax.broadcasted_iota(jnp.int32, (tq, 1), 0)
    blk = lax.broadcasted_iota(jnp.int32, (1, nc), 1)
    cmask = blk * CMP_BLOCK + (CMP_BLOCK - 1) <= t
    cur = t // SEL_BLOCK
    forced = (blk == 0) | (blk == cur) | (blk == cur - 1)
    future = blk > cur
    kc = kc_ref[0].astype(BF16)
    vc = vc_ref[0].astype(BF16)
    scores = []
    for g in range(NSA_KV_HEADS):
        gl = slice(g * NSA_HEAD_DIM, (g + 1) * NSA_HEAD_DIM)
        psum = jnp.zeros((tq, nc), F32)
        for r in range(NSA_GROUP):
            h = g * NSA_GROUP + r
            hl = slice(h * NSA_HEAD_DIM, (h + 1) * NSA_HEAD_DIM)
            p = _masked_softmax(_nt(q_ref[:, hl], kc[:, gl]), cmask)
            psum = psum + p
            c = GATE_LANE0 + 3 * h
            o_ref[:, hl] = _mm(p.astype(BF16), vc[:, gl]) * gate_ref[:, c:c + 1]
        score = jnp.where(forced, float(NSA_GROUP + 1), psum)
        scores.append(jnp.where(future, -1.0, score))
    sel = _top_blocks(jnp.concatenate(scores, axis=0), blk, min(TOP_N, nc))
    for g in range(NSA_KV_HEADS):
        keep = jnp.logical_and(sel[g * tq:(g + 1) * tq], jnp.logical_not(future))
        selt_ref[0, g] = jnp.where(keep, 0.0, NEG).T


def _cmp_prompt(nq, kcc, vcc, misc, *, batch, seq, tq):
    nc = seq // CMP_BLOCK
    nt = seq // tq
    row = lambda n: pl.BlockSpec((tq, n), lambda b, i: (b * nt + i, 0))
    cblk = pl.BlockSpec((1, nc, KV_W), lambda b, i: (b, 0, 0))
    return pl.pallas_call(
        functools.partial(_cmp_kernel, tq=tq, nc=nc),
        grid=(batch, nt),
        in_specs=[row(NSA_WIDTH), cblk, cblk, row(LANE)],
        out_specs=[row(NSA_WIDTH),
                   pl.BlockSpec((1, NSA_KV_HEADS, nc, tq), lambda b, i: (b, 0, 0, i))],
        out_shape=[jax.ShapeDtypeStruct((batch * seq, NSA_WIDTH), F32),
                   jax.ShapeDtypeStruct((batch, NSA_KV_HEADS, nc, seq), F32)],
        compiler_params=pltpu.CompilerParams(dimension_semantics=("arbitrary", "arbitrary"),
                                             vmem_limit_bytes=VMEM_LIMIT),
        name="nsa_cmp_prompt",
    )(nq, kcc.reshape(batch, nc, KV_W), vcc.reshape(batch, nc, KV_W), misc)


AUG_ROWS = NSA_HEAD_DIM + SUBLANE
BF16_ROWS = 2 * SUBLANE


def _slc_kernel(q_ref, ocmp_ref, gate_ref, selt_ref, e0t_ref, ks_ref, vs_ref, kw_ref, vw_ref, o_ref,
                q4t_scr, acc_scr, comb_scr, *, tq, tk):
    q0 = pl.program_id(1) * tq
    n_kt = (q0 + tq - 1) // tk + 1
    t = q0 + lax.broadcasted_iota(jnp.int32, (1, tq), 1)
    bpt = tk // SEL_BLOCK
    wlen = WINDOW + tq
    cols = NSA_GROUP * tq
    w0 = pl.multiple_of(jnp.maximum(q0 - WINDOW, 0), tq)
    wpos = w0 + lax.broadcasted_iota(jnp.int32, (wlen, 1), 0)
    wbias = jnp.where((wpos <= t) & (t - wpos < WINDOW), 0.0, NEG)
    qt = q_ref[...].astype(F32).T
    gate_t = gate_ref[...].T
    zero = jnp.zeros((NSA_HEAD_DIM, cols), BF16)

    def head_cols(a):
        return jnp.concatenate([a] * NSA_GROUP, axis=1)

    def with_ones(vt):
        return jnp.concatenate([vt, jnp.ones((SUBLANE, vt.shape[1]), BF16)], axis=0)

    groups = range(NSA_KV_HEADS)
    lanes = [slice(g * NSA_HEAD_DIM, (g + 1) * NSA_HEAD_DIM) for g in groups]
    for g in groups:
        q4t = jnp.concatenate(
            [qt[(g * NSA_GROUP + r) * NSA_HEAD_DIM:(g * NSA_GROUP + r + 1) * NSA_HEAD_DIM]
             for r in range(NSA_GROUP)], axis=1).astype(BF16)
        q4t_scr[g] = jnp.concatenate([q4t, zero] if g == 0 else [zero, q4t], axis=0)
    acc_scr[...] = jnp.zeros(acc_scr.shape, F32)

    own_lanes = [(lax.broadcasted_iota(jnp.int32, (1, KV_W), 1) // NSA_HEAD_DIM) == g for g in groups]
    pad_rows = jnp.zeros((BF16_ROWS - bpt, cols), BF16)

    def tile(j, m_old, diagonal):
        k0 = pl.multiple_of(j * tk, tk)
        k_rows = ks_ref[0, pl.ds(k0, tk), :]
        m_out = []
        for g in groups:
            sb = selt_ref[0, g, pl.ds(pl.multiple_of(j * bpt, bpt), bpt), :]
            og = 1 - g
            q4t_scr[g, og * NSA_HEAD_DIM:og * NSA_HEAD_DIM + BF16_ROWS, :] = jnp.concatenate(
                [head_cols(sb).astype(BF16), pad_rows], axis=0)
            s = _mm(jnp.where(own_lanes[g], k_rows, e0t_ref[g]), q4t_scr[g])
            if diagonal:
                kpos = k0 + lax.broadcasted_iota(jnp.int32, (tk, 1), 0)
                s = jnp.where(kpos <= head_cols(t), s, NEG)
            m_new = jnp.maximum(m_old[g], jnp.max(s, axis=0, keepdims=True))
            p = jnp.exp(s - m_new).astype(BF16)
            acc_scr[g] = (jnp.exp(m_old[g] - m_new) * acc_scr[g]
                          + _mm(with_ones(vs_ref[0, lanes[g], pl.ds(k0, tk)]), p))
            m_out.append(m_new)
        return tuple(m_out)

    m = lax.fori_loop(0, n_kt - 1, lambda j, m: tile(j, m, False),
                      (jnp.full((1, cols), NEG, F32),) * NSA_KV_HEADS)
    tile(n_kt - 1, m, True)

    kw_rows = kw_ref[0, pl.ds(w0, wlen), :]
    for g in groups:
        acc = acc_scr[g]
        o_slc = acc[0:NSA_HEAD_DIM] / acc[NSA_HEAD_DIM:NSA_HEAD_DIM + 1]
        s = _mm(jnp.where(own_lanes[g], kw_rows, jnp.zeros_like(kw_rows)), q4t_scr[g]) + head_cols(wbias)
        p = jnp.exp(s - jnp.max(s, axis=0, keepdims=True)).astype(BF16)
        acc = _mm(with_ones(vw_ref[0, lanes[g], pl.ds(w0, wlen)]), p)
        o_win = acc[0:NSA_HEAD_DIM] / acc[NSA_HEAD_DIM:NSA_HEAD_DIM + 1]

        for r in range(NSA_GROUP):
            h = g * NSA_GROUP + r
            c = GATE_LANE0 + 3 * h
            cs = slice(r * tq, (r + 1) * tq)
            comb_scr[h * NSA_HEAD_DIM:(h + 1) * NSA_HEAD_DIM, :] = (
                o_slc[:, cs] * gate_t[c + 1:c + 2] + o_win[:, cs] * gate_t[c + 2:c + 3])
    o_ref[...] = ocmp_ref[...] + comb_scr[...].T


def _slc_prompt(nq, ocmp, misc, selt, ks, vs, kw, vw, *, batch, seq, tq, tk):
    nt = seq // tq
    nc = seq // SEL_BLOCK
    bpt = tk // SEL_BLOCK
    key_blk = jnp.arange(tk, dtype=jnp.int32)[None, :, None] // SEL_BLOCK
    other0 = (1 - jnp.arange(NSA_KV_HEADS, dtype=jnp.int32))[:, None, None] * NSA_HEAD_DIM
    e0t = (jnp.arange(KV_W, dtype=jnp.int32)[None, None, :] - other0 == key_blk).astype(BF16)
    assert tk % tq == 0 and seq >= WINDOW + tq and bpt <= BF16_ROWS
    cols = NSA_GROUP * tq
    row = lambda n: pl.BlockSpec((tq, n), lambda b, i: (b * nt + i, 0))
    k_rows = pl.BlockSpec((1, seq, KV_W), lambda b, i: (b, 0, 0), pipeline_mode=pl.Buffered(1))
    v_cols = pl.BlockSpec((1, KV_W, seq), lambda b, i: (b, 0, 0), pipeline_mode=pl.Buffered(1))
    return pl.pallas_call(
        functools.partial(_slc_kernel, tq=tq, tk=tk),
        grid=(batch, nt),
        in_specs=[row(NSA_WIDTH), row(NSA_WIDTH), row(LANE),
                  pl.BlockSpec((1, NSA_KV_HEADS, nc, tq), lambda b, i: (b, 0, 0, i)),
                  _const_spec((NSA_KV_HEADS, tk, KV_W)), k_rows, v_cols, k_rows, v_cols],
        out_specs=row(NSA_WIDTH),
        out_shape=jax.ShapeDtypeStruct((batch * seq, NSA_WIDTH), F32),
        scratch_shapes=[pltpu.VMEM((NSA_KV_HEADS, KV_W, cols), BF16),
                        pltpu.VMEM((NSA_KV_HEADS, AUG_ROWS, cols), F32),
                        pltpu.VMEM((NSA_WIDTH, tq), F32)],
        compiler_params=pltpu.CompilerParams(dimension_semantics=("arbitrary", "arbitrary"),
                                             vmem_limit_bytes=VMEM_LIMIT),
        name="nsa_slc_prompt",
    )(nq, ocmp, misc, selt, e0t, ks.reshape(batch, seq, KV_W), vs, kw.reshape(batch, seq, KV_W), vw)


PAGES_PER_MM = SUBLANE // (PAGE_SIZE // CMP_BLOCK)


def _pool_cmp_kernel(pk_ref, pv_ref, wk_ref, wv_ref, sel_ref, ok_ref, ov_ref, *, pp):
    def quad(i, carry):
        for p_ref, w_ref, o_ref in ((pk_ref, wk_ref, ok_ref), (pv_ref, wv_ref, ov_ref)):
            acc = jnp.zeros((SUBLANE, KV_W), F32)
            for k in range(PAGES_PER_MM):
                y = (p_ref[i * PAGES_PER_MM + k].reshape(KV_W, PAGE_SIZE) * w_ref[...]).astype(BF16)
                acc = acc + _nt(sel_ref[k], y)
            o_ref[pl.ds(pl.multiple_of(i * SUBLANE, SUBLANE), SUBLANE), :] = acc
        return carry

    lax.fori_loop(0, pp // PAGES_PER_MM, quad, 0)


def _pool_compress(pool_k, pool_v, wck_t, wcv_t, *, pp):
    n_pool = pool_k.shape[0]
    bpp = PAGE_SIZE // CMP_BLOCK
    rowid = jnp.arange(SUBLANE, dtype=jnp.int32)[None, :, None]
    want = (bpp * jnp.arange(PAGES_PER_MM, dtype=jnp.int32)[:, None, None]
            + jnp.arange(PAGE_SIZE, dtype=jnp.int32)[None, None, :] // CMP_BLOCK)
    sel = (rowid == want).astype(BF16)
    pg = pl.BlockSpec((pp, NSA_KV_HEADS, NSA_HEAD_DIM, PAGE_SIZE), lambda i: (i, 0, 0, 0))
    og = pl.BlockSpec((pp * bpp, KV_W), lambda i: (i, 0))
    return pl.pallas_call(
        functools.partial(_pool_cmp_kernel, pp=pp),
        grid=(n_pool // pp,),
        in_specs=[pg, pg, _const_spec((KV_W, PAGE_SIZE)), _const_spec((KV_W, PAGE_SIZE)),
                  _const_spec(sel.shape)],
        out_specs=[og, og],
        out_shape=[jax.ShapeDtypeStruct((n_pool * bpp, KV_W), F32)] * 2,
        compiler_params=pltpu.CompilerParams(dimension_semantics=("arbitrary",),
                                             vmem_limit_bytes=VMEM_LIMIT),
        name="pool_compress",
    )(pool_k, pool_v, wck_t, wcv_t, sel)


def _cmp_step_kernel(pt_ref, q_ref, kcn_ref, vcn_ref, gate_ref, wk_ref, wv_ref, pkc_ref, pvc_ref,
                     o_ref, idx_ref, kbuf, vbuf, *, n_pages, ncp, bb):
    b0 = pl.program_id(0) * bb
    bpp = PAGE_SIZE // CMP_BLOCK
    n_past = n_pages * bpp
    t = n_pages * PAGE_SIZE
    blk = lax.broadcasted_iota(jnp.int32, (1, ncp), 1)
    cmask = blk * CMP_BLOCK + (CMP_BLOCK - 1) <= t
    cur = t // SEL_BLOCK
    forced = (blk == 0) | (blk == cur) | (blk == cur - 1)
    scores = []
    for i in range(bb):
        def gather(j, carry):
            p = pt_ref[b0 + i, j]
            for src, dst in ((pkc_ref, kbuf), (pvc_ref, vbuf)):
                for u in range(bpp):
                    dst[i, pl.ds(j * bpp + u, 1), :] = src[pl.ds(p * bpp + u, 1), :]
            return carry

        lax.fori_loop(0, n_pages, gather, 0)
        kbuf[i, n_past:ncp, :] = jnp.zeros((ncp - n_past, KV_W), F32)
        vbuf[i, n_past:ncp, :] = jnp.zeros((ncp - n_past, KV_W), F32)
        kbuf[i, n_past:n_past + 1, :] = kcn_ref[i:i + 1, :] * wk_ref[0:1, :]
        vbuf[i, n_past:n_past + 1, :] = vcn_ref[i:i + 1, :] * wv_ref[0:1, :]
        kc = kbuf[i].astype(BF16)
        vc = vbuf[i].astype(BF16)
        qrow = q_ref[i:i + 1, :]
        for g in range(NSA_KV_HEADS):
            gl = slice(g * NSA_HEAD_DIM, (g + 1) * NSA_HEAD_DIM)
            p = _masked_softmax(_nt(_group_rows(qrow, g), kc[:, gl]), cmask)
            o4 = _mm(p.astype(BF16), vc[:, gl])
            for r in range(NSA_GROUP):
                h = g * NSA_GROUP + r
                c = GATE_LANE0 + 3 * h
                o_ref[i:i + 1, h * NSA_HEAD_DIM:(h + 1) * NSA_HEAD_DIM] = (
                    o4[r:r + 1] * gate_ref[i:i + 1, c:c + 1])
            score = jnp.where(forced, float(NSA_GROUP + 1),
                              jnp.sum(p[0:NSA_GROUP], axis=0, keepdims=True))
            score = jnp.where(blk > cur, -1.0, score)
            scores.append(jnp.where(blk > n_past, -3.0, score))
    score = jnp.concatenate(scores, axis=0)
    lane = lax.broadcasted_iota(jnp.int32, (1, LANE), 1)
    idx = jnp.zeros((bb * NSA_KV_HEADS, LANE), jnp.int32)
    for n in range(TOP_N):
        j = _first_max(score, blk)
        idx = jnp.where(lane == n, j, idx)
        score = jnp.where(blk == j, -4.0, score)
    idx_ref[...] = idx


def _cmp_step(page_table, nq, kcn, vcn, misc, wck, wcv, pkc, pvc, *, bb):
    n, n_pages = page_table.shape
    bpp = PAGE_SIZE // CMP_BLOCK
    ncp = -(-(n_pages * bpp + 1) // LANE) * LANE
    row = lambda w: pl.BlockSpec((bb, w), lambda i, pt: (i, 0))
    cst = lambda shape: pl.BlockSpec(shape, lambda i, pt: (0, 0), pipeline_mode=pl.Buffered(1))
    return pl.pallas_call(
        functools.partial(_cmp_step_kernel, n_pages=n_pages, ncp=ncp, bb=bb),
        grid_spec=pltpu.PrefetchScalarGridSpec(
            num_scalar_prefetch=1, grid=(n // bb,),
            in_specs=[row(NSA_WIDTH), row(KV_W), row(KV_W), row(LANE),
                      cst((CMP_BLOCK, KV_W)), cst((CMP_BLOCK, KV_W)),
                      cst(pkc.shape), cst(pvc.shape)],
            out_specs=[row(NSA_WIDTH), pl.BlockSpec((bb * NSA_KV_HEADS, LANE), lambda i, pt: (i, 0))],
            scratch_shapes=[pltpu.VMEM((bb, ncp, KV_W), F32), pltpu.VMEM((bb, ncp, KV_W), F32)]),
        out_shape=[jax.ShapeDtypeStruct((n, NSA_WIDTH), F32),
                   jax.ShapeDtypeStruct((n * NSA_KV_HEADS, LANE), jnp.int32)],
        compiler_params=pltpu.CompilerParams(dimension_semantics=("arbitrary",),
                                             vmem_limit_bytes=VMEM_LIMIT),
        name="nsa_cmp_step",
    )(page_table, nq.astype(F32), kcn, vcn, misc, wck, wcv, pkc, pvc)


def _slc_step_kernel(pt_ref, idx_ref, q_ref, ocmp_ref, gate_ref, ksn_ref, vsn_ref, kwn_ref, vwn_ref,
                     kwb_ref, vwb_ref, pks_ref, pvs_ref, o_ref, kbuf, vbuf, sem, *, n_pages, wb):
    b = pl.program_id(0)
    bpp = PAGE_SIZE // SEL_BLOCK
    n_past = n_pages * bpp
    t = n_pages * PAGE_SIZE
    n_sel = NSA_KV_HEADS * TOP_N

    def copies(i):
        page = pt_ref[b, jnp.minimum(idx_ref[b, i], n_past - 1) // bpp]
        g = i // TOP_N
        return (pltpu.make_async_copy(pks_ref.at[page, g], kbuf.at[i], sem.at[0, i]),
                pltpu.make_async_copy(pvs_ref.at[page, g], vbuf.at[i], sem.at[1, i]))

    for i in range(n_sel):
        for cp in copies(i):
            cp.start()
    for i in range(n_sel):
        for cp in copies(i):
            cp.wait()

    kidx = lax.broadcasted_iota(jnp.int32, (1, TOP_N * PAGE_SIZE), 1)
    krow = kidx % PAGE_SIZE
    wpos = t - wb + lax.broadcasted_iota(jnp.int32, (1, wb), 1)
    wmask = (wpos >= 0) & (wpos <= t) & (t - wpos < WINDOW)
    qrow = q_ref[0].astype(F32)

    def with_new_key(s_old, mask_old, vt_old, k_new, v_new, q4, use_new):
        sn = jnp.sum(q4.astype(F32) * k_new.astype(BF16).astype(F32), axis=-1, keepdims=True)
        sn = jnp.where(use_new, sn, NEG)
        so = jnp.where(mask_old, s_old, NEG)
        m = jnp.maximum(jnp.max(so, axis=-1, keepdims=True), sn)
        eo = jnp.where(mask_old, jnp.exp(so - m), 0.0)
        en = jnp.where(use_new, jnp.exp(sn - m), 0.0)
        den = jnp.maximum(jnp.sum(eo, axis=-1, keepdims=True) + en, 1e-30)
        return (_nt((eo / den).astype(BF16), vt_old)
                + (en / den).astype(BF16).astype(F32) * v_new.astype(BF16).astype(F32))

    for g in range(NSA_KV_HEADS):
        gl = slice(g * NSA_HEAD_DIM, (g + 1) * NSA_HEAD_DIM)
        q4 = _group_rows(qrow, g)
        kblk = jnp.zeros((1, TOP_N * PAGE_SIZE), jnp.int32)
        has_new = False
        for n in range(TOP_N):
            blk = idx_ref[b, g * TOP_N + n]
            has_new = jnp.logical_or(has_new, blk >= n_past)
            kblk = jnp.where(kidx // PAGE_SIZE == n, blk, kblk)
        valid = ((krow // SEL_BLOCK == kblk % bpp) & (kblk < n_past)
                 & ((kblk // bpp) * PAGE_SIZE + krow <= t))
        kt = jnp.concatenate([kbuf[g * TOP_N + n] for n in range(TOP_N)], axis=1).astype(BF16)
        vt = jnp.concatenate([vbuf[g * TOP_N + n] for n in range(TOP_N)], axis=1).astype(BF16)
        o_slc = with_new_key(_mm(q4, kt), valid, vt, ksn_ref[0, :, gl], vsn_ref[0, :, gl], q4, has_new)
        o_win = with_new_key(_mm(q4, kwb_ref[0, g].astype(BF16)), wmask, vwb_ref[0, g].astype(BF16),
                             kwn_ref[0, :, gl], vwn_ref[0, :, gl], q4, True)
        for r in range(NSA_GROUP):
            h = g * NSA_GROUP + r
            hl = slice(h * NSA_HEAD_DIM, (h + 1) * NSA_HEAD_DIM)
            c = GATE_LANE0 + 3 * h
            o_ref[0, :, hl] = (ocmp_ref[0, :, hl] + o_slc[r:r + 1] * gate_ref[0, :, c + 1:c + 2]
                               + o_win[r:r + 1] * gate_ref[0, :, c + 2:c + 3])


def _slc_step(page_table, idx, nq, ocmp, misc, ksn, vsn, kwn, vwn, buf_kw, buf_vw, pool_ks, pool_vs):
    n, n_pages = page_table.shape
    wb = buf_kw.shape[-1]
    n_sel = NSA_KV_HEADS * TOP_N
    r3 = lambda a: a.reshape(n, 1, a.shape[-1])
    one = lambda w: pl.BlockSpec((1, 1, w), lambda i, pt, ix: (i, 0, 0))
    win = pl.BlockSpec((1, NSA_KV_HEADS, NSA_HEAD_DIM, wb), lambda i, pt, ix: (i, 0, 0, 0))
    hbm = pl.BlockSpec(memory_space=pl.ANY)
    return pl.pallas_call(
        functools.partial(_slc_step_kernel, n_pages=n_pages, wb=wb),
        grid_spec=pltpu.PrefetchScalarGridSpec(
            num_scalar_prefetch=2, grid=(n,),
            in_specs=[one(NSA_WIDTH), one(NSA_WIDTH), one(LANE), one(KV_W), one(KV_W), one(KV_W),
                      one(KV_W), win, win, hbm, hbm],
            out_specs=one(NSA_WIDTH),
            scratch_shapes=[pltpu.VMEM((n_sel, NSA_HEAD_DIM, PAGE_SIZE), F32),
                            pltpu.VMEM((n_sel, NSA_HEAD_DIM, PAGE_SIZE), F32),
                            pltpu.SemaphoreType.DMA((2, n_sel))]),
        out_shape=jax.ShapeDtypeStruct((n, 1, NSA_WIDTH), F32),
        compiler_params=pltpu.CompilerParams(dimension_semantics=("arbitrary",),
                                             vmem_limit_bytes=VMEM_LIMIT),
        name="nsa_slc_step",
    )(page_table, idx, r3(nq), r3(ocmp), r3(misc), r3(ksn), r3(vsn), r3(kwn), r3(vwn),
      buf_kw, buf_vw, pool_ks, pool_vs)


def _post_kernel(x_ref, og_ref, on_ref, pe_ref, p0_ref, p1_ref, wo_ref, gf_ref, wup_ref, wcv_ref,
                 bcv_ref, wdn_ref, gp_ref, wple_ref, wpg_ref, gfin_ref, y_ref, ut_ref,
                 carry, ext, *, tm, tail, shift):
    attn = (_mm(og_ref[...].astype(BF16), wo_ref[0:GLA_WIDTH, :])
            + _mm(on_ref[...].astype(BF16), wo_ref[GLA_WIDTH:, :]))
    h = x_ref[...] + attn
    hn = _rms(h, gf_ref[...]).astype(BF16)
    if shift:
        @pl.when(pl.program_id(1) == 0)
        def _():
            carry[...] = jnp.zeros_like(carry)

    def chunk(j, acc):
        u = _mm(hn, wup_ref[j])
        wc = wcv_ref[j]
        if shift:
            ext[0:SUBLANE, :] = carry[j]
            ext[SUBLANE:, :] = u
            um2 = ext[pl.ds(SUBLANE - 2, tm), :]
            um1 = ext[pl.ds(SUBLANE - 1, tm), :]
            carry[j] = u[tm - SUBLANE:, :]
        else:
            um2 = p0_ref[j]
            um1 = p1_ref[j]
        c = bcv_ref[j] + um2 * wc[0:1] + um1 * wc[1:2] + u * wc[2:3]
        ut_ref[0, j] = u[tm - tail:, :]
        a = c[:, :FF_CHUNK]
        act = (a * _sigmoid(a) * c[:, FF_CHUNK:]).astype(BF16)
        return acc + _mm(act, wdn_ref[j])

    h = h + lax.fori_loop(0, N_FF_CHUNKS, chunk, jnp.zeros((tm, D_MODEL), F32))
    gate = _sigmoid(_mm(_rms(h, gp_ref[...]).astype(BF16), wpg_ref[...]))
    h = h + _mm(pe_ref[...].astype(BF16), wple_ref[...]) * gate
    y_ref[...] = _rms(h, gfin_ref[...])


def _post(x2d, ogla, onsa, pe2d, prev, pw, g_final, *, batch, tm, shift):
    m = x2d.shape[0]
    nt = m // (batch * tm)
    tail = SUBLANE if shift else tm
    row = lambda n: pl.BlockSpec((tm, n), lambda b, i: (b * nt + i, 0))
    cst = lambda shape: pl.BlockSpec(shape, lambda b, i: (0,) * len(shape),
                                     pipeline_mode=pl.Buffered(1))
    if shift:
        prev_spec = cst((1, 1, LANE))
        p0 = p1 = jnp.zeros((1, 1, LANE), F32)
    else:
        prev_spec = pl.BlockSpec((N_FF_CHUNKS, tm, 2 * FF_CHUNK), lambda b, i: (0, b * nt + i, 0))
        p0, p1 = prev
    return pl.pallas_call(
        functools.partial(_post_kernel, tm=tm, tail=tail, shift=shift),
        grid=(batch, nt),
        in_specs=[row(D_MODEL), row(GLA_WIDTH), row(NSA_WIDTH), row(PLE_DIM), prev_spec, prev_spec,
                  cst((D_MODEL, D_MODEL)), cst((1, D_MODEL)),
                  cst((N_FF_CHUNKS, D_MODEL, 2 * FF_CHUNK)), cst((N_FF_CHUNKS, CONV_W, 2 * FF_CHUNK)),
                  cst((N_FF_CHUNKS, 1, 2 * FF_CHUNK)), cst((N_FF_CHUNKS, FF_CHUNK, D_MODEL)),
                  cst((1, D_MODEL)), cst((PLE_DIM, D_MODEL)), cst((D_MODEL, D_MODEL)),
                  cst((1, D_MODEL))],
        out_specs=[row(D_MODEL),
                   pl.BlockSpec((1, N_FF_CHUNKS, tail, 2 * FF_CHUNK), lambda b, i: (b, 0, 0, 0))],
        out_shape=[jax.ShapeDtypeStruct((m, D_MODEL), F32),
                   jax.ShapeDtypeStruct((batch, N_FF_CHUNKS, tail, 2 * FF_CHUNK), F32)],
        scratch_shapes=[pltpu.VMEM((N_FF_CHUNKS, SUBLANE, 2 * FF_CHUNK), F32),
                        pltpu.VMEM((tm + SUBLANE, 2 * FF_CHUNK), F32)],
        compiler_params=pltpu.CompilerParams(dimension_semantics=("arbitrary", "arbitrary"),
                                             vmem_limit_bytes=VMEM_LIMIT),
        name="post",
    )(x2d, ogla, onsa, pe2d, p0, p1, pw["w_o"], pw["g_ffn"], pw["w_up"], pw["w_conv"], pw["b_conv"],
      pw["w_down"], pw["g_ple"], pw["w_ple"], pw["w_ple_gate"], g_final)


def _ff_chunked(a):
    lead = a.shape[:-1]
    a = a.reshape(*lead, 2, N_FF_CHUNKS, FF_CHUNK)
    a = jnp.moveaxis(a, -2, 0)
    return a.reshape(N_FF_CHUNKS, *lead, 2 * FF_CHUNK)


def _ff_unchunked(a):
    r = a.shape[1]
    return a.reshape(N_FF_CHUNKS, r, 2, FF_CHUNK).transpose(1, 2, 0, 3).reshape(r, 2 * D_FF)


def _pack_weights(i, g_attn, w_in, w_gla_gate, b_gla_gate, g_gla_out, b_nsa_gate, w_cmp_k, w_cmp_v,
                  w_o, g_ffn, w_up, w_conv, b_conv, w_down, g_ple, w_ple, w_ple_gate):
    w = w_in[i]
    o_glr = 2 * GLA_QK + 2 * GLA_WIDTH
    o_nq = o_glr + GLA_GATE_RANK
    o_ng = o_nq + NSA_WIDTH + 6 * KV_W
    pad = jnp.zeros((D_MODEL, LANE - GLA_GATE_RANK - 3 * NSA_HEADS), w.dtype)
    w_pack = jnp.concatenate([w[:, :o_glr], w[:, o_nq:o_ng], w[:, o_glr:o_nq], w[:, o_ng:], pad], axis=1)
    w_gate = jnp.zeros((LANE, GLA_QK), F32).at[:GLA_GATE_RANK].set(w_gla_gate[i])
    b_misc = jnp.zeros((1, LANE), F32).at[0, GATE_LANE0:GATE_LANE0 + 3 * NSA_HEADS].set(b_nsa_gate[i])
    return {
        "g_attn": g_attn[i][None], "w_in": w_pack.astype(BF16), "w_gate": w_gate.astype(BF16),
        "b_gate": b_gla_gate[i][None], "b_misc": b_misc, "g_gla_out": g_gla_out[i][None],
        "w_ck": jnp.repeat(w_cmp_k[i], NSA_HEAD_DIM, axis=1),
        "w_cv": jnp.repeat(w_cmp_v[i], NSA_HEAD_DIM, axis=1),
        "w_ck_t": jnp.tile(jnp.repeat(w_cmp_k[i].T, NSA_HEAD_DIM, axis=0), (1, PAGE_SIZE // CMP_BLOCK)),
        "w_cv_t": jnp.tile(jnp.repeat(w_cmp_v[i].T, NSA_HEAD_DIM, axis=0), (1, PAGE_SIZE // CMP_BLOCK)),
        "w_o": w_o[i].astype(BF16), "g_ffn": g_ffn[i][None],
        "w_up": _ff_chunked(w_up[i]).astype(BF16), "w_conv": _ff_chunked(w_conv[i]),
        "b_conv": _ff_chunked(b_conv[i][None]),
        "w_down": w_down[i].reshape(N_FF_CHUNKS, FF_CHUNK, D_MODEL).astype(BF16),
        "g_ple": g_ple[i][None], "w_ple": w_ple[i].astype(BF16),
        "w_ple_gate": w_ple_gate[i].astype(BF16),
    }


def _rope_tables(pos):
    half = ROT_DIM // 2
    inv = ROPE_THETA ** (-jnp.arange(half, dtype=F32) / half)
    ang = pos.astype(F32)[:, None] * inv[None, :]
    d = jnp.arange(LANE) % NSA_HEAD_DIM
    cos = jnp.cos(ang)[:, d % half]
    sin = jnp.sin(ang)[:, d % half]
    return (jnp.where(d < ROT_DIM, cos, 1.0), jnp.where(d < half, -sin, 0.0),
            jnp.where((d >= half) & (d < ROT_DIM), sin, 0.0))


def _kv4(a, b, t):
    return a.reshape(b, t, NSA_KV_HEADS, NSA_HEAD_DIM)


def _prompt_layer(x, pe, pw, g_final):
    b, t, _ = x.shape
    m = b * t
    tabs = _rope_tables(jnp.arange(t, dtype=jnp.int32))
    (gqk, gv, gr, la, nq, misc, kc, vc, ks, vs, kw, vw, ks16, vs16, kw16, vw16, kcc, vcc) = _inproj(
        x.reshape(m, D_MODEL), tabs, pw, tm=512, batch=b)
    ogla, h_gla = _gla_prompt(gqk, gv, gr, la, pw["g_gla_out"], batch=b, seq=t, tg=512)
    ocmp, selt = _cmp_prompt(nq, kcc, vcc, misc, batch=b, seq=t, tq=256)
    onsa = _slc_prompt(nq, ocmp, misc, selt, ks16, vs16, kw16, vw16, batch=b, seq=t, tq=128, tk=512)
    y, utail = _post(x.reshape(m, D_MODEL), ogla, onsa, pe.reshape(m, PLE_DIM), None, pw, g_final,
                     batch=b, tm=512, shift=True)
    conv_new = jax.vmap(_ff_unchunked)(utail)[:, SUBLANE - (CONV_W - 1):]
    nw = min(WINDOW, t)

    def rows(a):
        return a.reshape(b, NSA_KV_HEADS, NSA_HEAD_DIM, t).transpose(0, 3, 1, 2)

    state = (rows(kc), rows(vc), rows(ks), rows(vs), rows(kw)[:, t - nw:], rows(vw)[:, t - nw:],
             h_gla, conv_new)
    return y.reshape(b, t, D_MODEL), state


def _sample_layer(x, pe, pw, g_final, pools, bufs, h0, conv_prev, page_table):
    n, t, _ = x.shape
    assert t == 1, "the sample group is written for one new token per request"
    n_pages = page_table.shape[1]
    pos = jnp.full((n,), n_pages * PAGE_SIZE, jnp.int32)
    tabs = _rope_tables(pos)
    x2d = x.reshape(n, D_MODEL)
    (gqk, gv, gr, la, nq, misc, kc, vc, ks, vs, kw, vw) = _inproj(x2d, tabs, pw, tm=n)
    ogla, h_gla = _gla_step(gqk, gv, gr, la, pw["g_gla_out"], h0, bb=SUBLANE)
    pool_kc, pool_vc, pool_ks, pool_vs = [p.transpose(0, 2, 3, 1) for p in pools]
    n_pool = pool_kc.shape[0]
    pp = max(p for p in range(PAGES_PER_MM, 65, PAGES_PER_MM) if n_pool % p == 0)
    pkc, pvc = _pool_compress(pool_kc, pool_vc, pw["w_ck_t"], pw["w_cv_t"], pp=pp)
    ocmp, idx = _cmp_step(page_table, nq, kc, vc, misc, pw["w_ck"], pw["w_cv"], pkc, pvc, bb=SUBLANE)
    idx = idx.reshape(n, NSA_KV_HEADS, LANE)[:, :, :TOP_N].reshape(n, NSA_KV_HEADS * TOP_N)
    buf_kw, buf_vw = bufs
    onsa = _slc_step(page_table, idx, nq, ocmp, misc, ks, vs, kw, vw,
                     buf_kw.transpose(0, 2, 3, 1), buf_vw.transpose(0, 2, 3, 1), pool_ks, pool_vs)
    prev = (_ff_chunked(conv_prev[:, 0]), _ff_chunked(conv_prev[:, 1]))
    y, utail = _post(x2d, ogla, onsa.reshape(n, NSA_WIDTH), pe.reshape(n, PLE_DIM), prev, pw, g_final,
                     batch=1, tm=n, shift=False)
    u = _ff_unchunked(utail[0])
    conv_new = jnp.stack([conv_prev[:, 1], u], axis=1)
    wb = buf_kw.shape[1]
    nw = min(WINDOW, wb + 1)
    kwin = jnp.concatenate([buf_kw, _kv4(kw, n, 1)], axis=1)[:, wb + 1 - nw:]
    vwin = jnp.concatenate([buf_vw, _kv4(vw, n, 1)], axis=1)[:, wb + 1 - nw:]
    state = (_kv4(kc, n, 1), _kv4(vc, n, 1), _kv4(ks, n, 1), _kv4(vs, n, 1), kwin, vwin, h_gla,
             conv_new)
    return y.reshape(n, 1, D_MODEL), state


def kernel(x_prompt, x_sample, p_prompt, p_sample, cache_k_cmp, cache_v_cmp, cache_k_slc, cache_v_slc, cache_k_win, cache_v_win, state_gla, state_conv, page_table, g_attn, w_in, w_gla_gate, b_gla_gate, g_gla_out, b_nsa_gate, w_cmp_k, w_cmp_v, w_o, g_ffn, w_up, w_conv, b_conv, w_down, g_ple, w_ple, w_ple_gate, g_final):
    depth = w_in.shape[0]
    assert depth == 1, "the final norm is fused into the layer kernel; written for a one-layer trunk"
    pw = _pack_weights(0, g_attn, w_in, w_gla_gate, b_gla_gate, g_gla_out, b_nsa_gate, w_cmp_k,
                       w_cmp_v, w_o, g_ffn, w_up, w_conv, b_conv, w_down, g_ple, w_ple, w_ple_gate)
    gfin = g_final[None]
    y_p, st_p = _prompt_layer(x_prompt, p_prompt[0], pw, gfin)
    y_s, st_s = _sample_layer(
        x_sample, p_sample[0], pw, gfin,
        (cache_k_cmp[0], cache_v_cmp[0], cache_k_slc[0], cache_v_slc[0]),
        (cache_k_win[0], cache_v_win[0]), state_gla[0], state_conv[0], page_table)
    return (y_p, y_s, *[a[None] for a in st_p], *[a[None] for a in st_s])
```

```python
import functools

import jax
import jax.numpy as jnp
from jax import lax
from jax.experimental import pallas as pl
from jax.experimental.pallas import tpu as pltpu

F32 = jnp.float32
BF16 = jnp.bfloat16

D_MODEL = 1024
PAGE_SIZE = 128
GLA_WIDTH = 512
GLA_HEADS = 4
GLA_QK = 256
GLA_DK = 64
GLA_DV = 128
GLA_GATE_RANK = 16
GLA_TAU = 16.0
GLA_CHUNK = 64
NSA_WIDTH = 512
NSA_HEADS = 8
NSA_HEAD_DIM = 64
NSA_KV_HEADS = 2
NSA_GROUP = 4
KV_W = 128
CMP_BLOCK = 64
SEL_BLOCK = 64
TOP_N = 16
N_FORCED = 3
WINDOW = 512
ROT_DIM = 16
ROPE_THETA = 500000.0
D_FF = 2816
CONV_W = 3
PLE_DIM = 256
EPS = 1e-6

LANE = 128
SUBLANE = 8
NEG = -1e30
FF_CHUNK = 256
N_FF_CHUNKS = D_FF // FF_CHUNK
VMEM_LIMIT = 56 * 1024 * 1024

C_GQ, C_GK, C_GV, C_GR, C_NQ = 0, 256, 512, 1024, 1536
C_KC, C_VC, C_KS, C_VS, C_KW, C_VW, C_MISC = 2048, 2176, 2304, 2432, 2560, 2688, 2816
D_PACK = 2944
GATE_LANE0 = GLA_GATE_RANK


def _nt(a, b):
    return lax.dot_general(a, b, (((1,), (1,)), ((), ())), preferred_element_type=F32)


def _tn(a, b):
    return lax.dot_general(a, b, (((0,), (0,)), ((), ())), preferred_element_type=F32)


def _mm(a, b):
    return jnp.dot(a, b, preferred_element_type=F32)


def _rms(x, g):
    return x * lax.rsqrt(jnp.mean(x * x, axis=-1, keepdims=True) + EPS) * g


def _sigmoid(x):
    return 1.0 / (1.0 + jnp.exp(-x))


def _group_rows(qrow, g):
    parts = [qrow[:, (g * NSA_GROUP + r) * NSA_HEAD_DIM:(g * NSA_GROUP + r + 1) * NSA_HEAD_DIM]
             for r in range(NSA_GROUP)]
    parts.append(jnp.zeros((SUBLANE - NSA_GROUP, NSA_HEAD_DIM), F32))
    return jnp.concatenate(parts, axis=0).astype(BF16)


def _const_spec(shape):
    nd = len(shape)
    return pl.BlockSpec(shape, lambda *_: (0,) * nd, pipeline_mode=pl.Buffered(1))


def _inproj_kernel(x_ref, g_ref, w_ref, wgate_ref, bgate_ref, bmisc_ref, cos_ref, sa_ref, sb_ref,
                   wck_ref, wcv_ref, *out_refs, prompt):
    gqk_ref, gv_ref, gr_ref, la_ref, nq_ref, misc_ref = out_refs[:6]
    kv_refs = out_refs[6:12]
    xn = _rms(x_ref[...], g_ref[...]).astype(BF16)

    def proj(lo, hi):
        return _mm(xn, w_ref[:, lo:hi])

    cos, sa, sb = cos_ref[...], sa_ref[...], sb_ref[...]

    def rope(z):
        return z * cos + pltpu.roll(z, LANE - ROT_DIM // 2, 1) * sa + pltpu.roll(z, ROT_DIM // 2, 1) * sb

    gqk_ref[...] = proj(C_GQ, C_GV)
    gv_ref[...] = proj(C_GV, C_GR)
    gr_ref[...] = proj(C_GR, C_NQ)
    for j in range(NSA_WIDTH // LANE):
        z = proj(C_NQ + j * LANE, C_NQ + (j + 1) * LANE)
        nq_ref[:, j * LANE:(j + 1) * LANE] = (rope(z) * (NSA_HEAD_DIM ** -0.5)).astype(BF16)
    kv = [rope(proj(C_KC, C_VC)), proj(C_VC, C_KS), rope(proj(C_KS, C_VS)), proj(C_VS, C_KW),
          rope(proj(C_KW, C_VW)), proj(C_VW, C_MISC)]
    z = proj(C_MISC, D_PACK)
    misc_ref[...] = _sigmoid(z + bmisc_ref[...])
    pre = _mm(z.astype(BF16), wgate_ref[...]) + bgate_ref[...]
    la_ref[...] = (jnp.minimum(pre, 0.0) - jnp.log(1.0 + jnp.exp(-jnp.abs(pre)))) * (1.0 / GLA_TAU)
    if not prompt:
        for ref, a in zip(kv_refs, kv):
            ref[...] = a
        return
    kv_t = [a.T for a in kv]
    for ref, a in zip(kv_refs, kv_t):
        ref[0] = a
    ks16_ref, vs16_ref, kw16_ref, vw16_ref, kcc_ref, vcc_ref = out_refs[12:]
    ks16_ref[...] = kv[2].astype(BF16)
    kw16_ref[...] = kv[4].astype(BF16)
    vs16_ref[0] = kv_t[3].astype(BF16)
    vw16_ref[0] = kv_t[5].astype(BF16)
    nb = kv[0].shape[0] // CMP_BLOCK
    kcc_ref[...] = jnp.sum(kv[0].reshape(nb, CMP_BLOCK, KV_W) * wck_ref[...][None], axis=1)
    vcc_ref[...] = jnp.sum(kv[1].reshape(nb, CMP_BLOCK, KV_W) * wcv_ref[...][None], axis=1)


def _inproj(x2d, tabs, pw, *, tm, batch=None):
    m = x2d.shape[0]
    prompt = batch is not None
    n_tab = tabs[0].shape[0] // tm
    row = lambda n: pl.BlockSpec((tm, n), lambda i: (i, 0))
    tab = pl.BlockSpec((tm, LANE), lambda i: (i % n_tab, 0))
    widths = [2 * GLA_QK, GLA_WIDTH, GLA_WIDTH, GLA_QK, NSA_WIDTH, LANE]
    dtypes = [F32, F32, F32, F32, BF16, F32]
    out_shape = [jax.ShapeDtypeStruct((m, n), d) for n, d in zip(widths, dtypes)]
    out_specs = [row(n) for n in widths]
    if prompt:
        seq = m // batch
        nt = seq // tm
        kvt = pl.BlockSpec((1, KV_W, tm), lambda i: (i // nt, 0, i % nt))
        nb = tm // CMP_BLOCK
        rows16 = jax.ShapeDtypeStruct((m, KV_W), BF16)
        cols16 = jax.ShapeDtypeStruct((batch, KV_W, seq), BF16)
        out_shape += ([jax.ShapeDtypeStruct((batch, KV_W, seq), F32)] * 6
                      + [rows16, cols16, rows16, cols16]
                      + [jax.ShapeDtypeStruct((m // CMP_BLOCK, KV_W), F32)] * 2)
        out_specs += ([kvt] * 6 + [row(KV_W), kvt, row(KV_W), kvt]
                      + [pl.BlockSpec((nb, KV_W), lambda i: (i, 0))] * 2)
    else:
        out_shape += [jax.ShapeDtypeStruct((m, KV_W), F32)] * 6
        out_specs += [row(KV_W)] * 6
    return pl.pallas_call(
        functools.partial(_inproj_kernel, prompt=prompt),
        grid=(m // tm,),
        in_specs=[row(D_MODEL), _const_spec((1, D_MODEL)), _const_spec((D_MODEL, D_PACK)),
                  _const_spec((LANE, GLA_QK)), _const_spec((1, GLA_QK)), _const_spec((1, LANE)),
                  tab, tab, tab, _const_spec((CMP_BLOCK, KV_W)), _const_spec((CMP_BLOCK, KV_W))],
        out_specs=out_specs, out_shape=out_shape,
        compiler_params=pltpu.CompilerParams(dimension_semantics=("arbitrary",),
                                             vmem_limit_bytes=VMEM_LIMIT),
        name="inproj",
    )(x2d, pw["g_attn"], pw["w_in"], pw["w_gate"], pw["b_gate"], pw["b_misc"], *tabs,
      pw["w_ck"], pw["w_cv"])


def _gla_kernel(qk_ref, v_ref, r_ref, la_ref, g_ref, o_ref, hout_ref, ht_scr, o_scr, *, n_chunks):
    t = pl.program_id(1)

    @pl.when(t == 0)
    def _():
        ht_scr[...] = jnp.zeros_like(ht_scr)

    ri = lax.broadcasted_iota(jnp.int32, (GLA_CHUNK, GLA_CHUNK), 0)
    ci = lax.broadcasted_iota(jnp.int32, (GLA_CHUNK, GLA_CHUNK), 1)
    causal = ri >= ci
    tril = causal.astype(F32)

    def chunk(c, carry):
        r0 = pl.multiple_of(c * GLA_CHUNK, GLA_CHUNK)
        rows = pl.ds(r0, GLA_CHUNK)
        bc = jnp.dot(tril, la_ref[rows, :], preferred_element_type=F32,
                     precision=lax.Precision.HIGHEST)
        blast = bc[GLA_CHUNK - 1:GLA_CHUNK, :]
        q = qk_ref[rows, 0:GLA_QK]
        k = qk_ref[rows, GLA_QK:2 * GLA_QK]
        qe = (q * jnp.exp(bc) * (GLA_DK ** -0.5)).astype(BF16)
        ke = (k * jnp.exp(-bc)).astype(BF16)
        kd = (k * jnp.exp(blast - bc)).astype(BF16)
        decay = jnp.exp(blast)
        for h in range(GLA_HEADS):
            ks = slice(h * GLA_DK, (h + 1) * GLA_DK)
            vh = v_ref[rows, h * GLA_DV:(h + 1) * GLA_DV].astype(BF16)
            ht = ht_scr[h]
            att = jnp.where(causal, _nt(qe[:, ks], ke[:, ks]), 0.0).astype(BF16)
            o_scr[rows, h * GLA_DV:(h + 1) * GLA_DV] = _mm(att, vh) + _nt(qe[:, ks], ht.astype(BF16))
            ht_scr[h] = ht * decay[:, ks] + _tn(vh, kd[:, ks])
        return carry

    lax.fori_loop(0, n_chunks, chunk, 0)
    for h in range(GLA_HEADS):
        hs = slice(h * GLA_DV, (h + 1) * GLA_DV)
        r = r_ref[:, hs]
        o_ref[:, hs] = _rms(o_scr[:, hs], g_ref[:, hs]) * (r * _sigmoid(r))

    @pl.when(t == pl.num_programs(1) - 1)
    def _():
        for h in range(GLA_HEADS):
            hout_ref[0, h] = ht_scr[h].T


def _gla_prompt(gqk, gv, gr, la, g_out, *, batch, seq, tg):
    nt = seq // tg
    row = lambda n: pl.BlockSpec((tg, n), lambda b, t: (b * nt + t, 0))
    return pl.pallas_call(
        functools.partial(_gla_kernel, n_chunks=tg // GLA_CHUNK),
        grid=(batch, nt),
        in_specs=[row(2 * GLA_QK), row(GLA_WIDTH), row(GLA_WIDTH), row(GLA_QK),
                  pl.BlockSpec((1, GLA_WIDTH), lambda b, t: (0, 0))],
        out_specs=[row(GLA_WIDTH),
                   pl.BlockSpec((1, GLA_HEADS, GLA_DK, GLA_DV), lambda b, t: (b, 0, 0, 0))],
        out_shape=[jax.ShapeDtypeStruct((batch * seq, GLA_WIDTH), F32),
                   jax.ShapeDtypeStruct((batch, GLA_HEADS, GLA_DK, GLA_DV), F32)],
        scratch_shapes=[pltpu.VMEM((GLA_HEADS, GLA_DV, GLA_DK), F32),
                        pltpu.VMEM((tg, GLA_WIDTH), F32)],
        compiler_params=pltpu.CompilerParams(dimension_semantics=("arbitrary", "arbitrary"),
                                             vmem_limit_bytes=VMEM_LIMIT),
        name="gla_prompt",
    )(gqk, gv, gr, la, g_out)


def _gla_step_kernel(qt_ref, kt_ref, at_ref, v_ref, r_ref, g_ref, h0_ref, o_ref, h_ref, *, bb):
    for b in range(bb):
        for h in range(GLA_HEADS):
            ds_ = slice(h * GLA_DK, (h + 1) * GLA_DK)
            vs_ = slice(h * GLA_DV, (h + 1) * GLA_DV)
            qcol = qt_ref[0, ds_, b:b + 1] * (GLA_DK ** -0.5)
            kcol = kt_ref[0, ds_, b:b + 1]
            acol = jnp.exp(at_ref[0, ds_, b:b + 1])
            hn = acol * h0_ref[b, h] + kcol * v_ref[b:b + 1, vs_]
            h_ref[b, h] = hn
            o = jnp.sum(qcol * hn, axis=0, keepdims=True)
            r = r_ref[b:b + 1, vs_]
            o_ref[b:b + 1, vs_] = _rms(o, g_ref[:, vs_]) * (r * _sigmoid(r))


def _gla_step(gqk, gv, gr, la, g_out, h0, *, bb):
    n = gqk.shape[0]
    ns = n // bb

    def cols(a):
        return a.reshape(ns, bb, GLA_QK).transpose(0, 2, 1)

    colspec = pl.BlockSpec((1, GLA_QK, bb), lambda i: (i, 0, 0))
    row = pl.BlockSpec((bb, GLA_WIDTH), lambda i: (i, 0))
    st = pl.BlockSpec((bb, GLA_HEADS, GLA_DK, GLA_DV), lambda i: (i, 0, 0, 0))
    return pl.pallas_call(
        functools.partial(_gla_step_kernel, bb=bb),
        grid=(ns,),
        in_specs=[colspec, colspec, colspec, row, row,
                  pl.BlockSpec((1, GLA_WIDTH), lambda i: (0, 0)), st],
        out_specs=[row, st],
        out_shape=[jax.ShapeDtypeStruct((n, GLA_WIDTH), F32),
                   jax.ShapeDtypeStruct(h0.shape, F32)],
        compiler_params=pltpu.CompilerParams(dimension_semantics=("arbitrary",)),
        name="gla_step",
    )(cols(gqk[:, :GLA_QK]), cols(gqk[:, GLA_QK:]), cols(la), gv, gr, g_out, h0)


def _masked_softmax(s, mask):
    s = jnp.where(mask, s, NEG)
    m = jnp.max(s, axis=-1, keepdims=True)
    e = jnp.where(mask, jnp.exp(s - m), 0.0)
    return e / jnp.maximum(jnp.sum(e, axis=-1, keepdims=True), 1e-30)


def _first_max(score, blk):
    m = jnp.max(score, axis=-1, keepdims=True)
    return jnp.min(jnp.where(score == m, blk, score.shape[-1]), axis=-1, keepdims=True)


def _top_blocks(score, blk, n_pick):
    sel = jnp.zeros(score.shape, jnp.bool_)
    for _ in range(n_pick):
        pick = blk == _first_max(score, blk)
        sel = jnp.logical_or(sel, pick)
        score = jnp.where(pick, -2.0, score)
    return sel


def _cmp_kernel(q_ref, kc_ref, vc_ref, gate_ref, o_ref, selt_ref, *, tq, nc):
    q0 = pl.program_id(1) * tq
    t = q0 + lax.broadcasted_iota(jnp.int32, (tq, 1), 0)
    blk = lax.broadcasted_iota(jnp.int32, (1, nc), 1)
    cmask = blk * CMP_BLOCK + (CMP_BLOCK - 1) <= t
    cur = t // SEL_BLOCK
    forced = (blk == 0) | (blk == cur) | (blk == cur - 1)
    future = blk > cur
    kc = kc_ref[0].astype(BF16)
    vc = vc_ref[0].astype(BF16)
    scores = []
    for g in range(NSA_KV_HEADS):
        gl = slice(g * NSA_HEAD_DIM, (g + 1) * NSA_HEAD_DIM)
        psum = jnp.zeros((tq, nc), F32)
        for r in range(NSA_GROUP):
            h = g * NSA_GROUP + r
            hl = slice(h * NSA_HEAD_DIM, (h + 1) * NSA_HEAD_DIM)
            p = _masked_softmax(_nt(q_ref[:, hl], kc[:, gl]), cmask)
            psum = psum + p
            c = GATE_LANE0 + 3 * h
            o_ref[:, hl] = _mm(p.astype(BF16), vc[:, gl]) * gate_ref[:, c:c + 1]
        scores.append(jnp.where(future, -1.0, jnp.where(forced, -2.0, psum)))
    sel = _top_blocks(jnp.concatenate(scores, axis=0), blk, TOP_N - N_FORCED)
    for g in range(NSA_KV_HEADS):
        keep = jnp.logical_and(jnp.logical_or(sel[g * tq:(g + 1) * tq], forced),
                               jnp.logical_not(future))
        selt_ref[0, g] = jnp.where(keep, 0.0, NEG).T


def _cmp_prompt(nq, kcc, vcc, misc, *, batch, seq, tq):
    nc = seq // CMP_BLOCK
    assert nc >= TOP_N
    nt = seq // tq
    row = lambda n: pl.BlockSpec((tq, n), lambda b, i: (b * nt + i, 0))
    cblk = pl.BlockSpec((1, nc, KV_W), lambda b, i: (b, 0, 0))
    return pl.pallas_call(
        functools.partial(_cmp_kernel, tq=tq, nc=nc),
        grid=(batch, nt),
        in_specs=[row(NSA_WIDTH), cblk, cblk, row(LANE)],
        out_specs=[row(NSA_WIDTH),
                   pl.BlockSpec((1, NSA_KV_HEADS, nc, tq), lambda b, i: (b, 0, 0, i))],
        out_shape=[jax.ShapeDtypeStruct((batch * seq, NSA_WIDTH), F32),
                   jax.ShapeDtypeStruct((batch, NSA_KV_HEADS, nc, seq), F32)],
        compiler_params=pltpu.CompilerParams(dimension_semantics=("arbitrary", "arbitrary"),
                                             vmem_limit_bytes=VMEM_LIMIT),
        name="nsa_cmp_prompt",
    )(nq, kcc.reshape(batch, nc, KV_W), vcc.reshape(batch, nc, KV_W), misc)


AUG_ROWS = NSA_HEAD_DIM + SUBLANE
BF16_ROWS = 2 * SUBLANE


def _slc_kernel(q_ref, ocmp_ref, gate_ref, selt_ref, e0t_ref, ks_ref, vs_ref, kw_ref, vw_ref, o_ref,
                q4t_scr, acc_scr, m_scr, comb_scr, *, tq, tk):
    q0 = pl.program_id(1) * tq
    n_kt = (q0 + tq - 1) // tk + 1
    t = q0 + lax.broadcasted_iota(jnp.int32, (1, tq), 1)
    bpt = tk // SEL_BLOCK
    wlen = WINDOW + tq
    cols = NSA_GROUP * tq
    w0 = pl.multiple_of(jnp.maximum(q0 - WINDOW, 0), tq)
    wpos = w0 + lax.broadcasted_iota(jnp.int32, (wlen, 1), 0)
    wbias = jnp.where((wpos <= t) & (t - wpos < WINDOW), 0.0, NEG)
    qt = q_ref[...].astype(F32).T
    gate_t = gate_ref[...].T
    zero = jnp.zeros((NSA_HEAD_DIM, cols), BF16)

    def head_cols(a):
        return jnp.concatenate([a] * NSA_GROUP, axis=1)

    def with_ones(vt):
        return jnp.concatenate([vt, jnp.ones((SUBLANE, vt.shape[1]), BF16)], axis=0)

    groups = range(NSA_KV_HEADS)
    lanes = [slice(g * NSA_HEAD_DIM, (g + 1) * NSA_HEAD_DIM) for g in groups]
    for g in groups:
        q4t = jnp.concatenate(
            [qt[(g * NSA_GROUP + r) * NSA_HEAD_DIM:(g * NSA_GROUP + r + 1) * NSA_HEAD_DIM]
             for r in range(NSA_GROUP)], axis=1).astype(BF16)
        q4t_scr[g] = jnp.concatenate([q4t, zero] if g == 0 else [zero, q4t], axis=0)
    acc_scr[...] = jnp.zeros(acc_scr.shape, F32)

    own_lanes = [(lax.broadcasted_iota(jnp.int32, (1, KV_W), 1) // NSA_HEAD_DIM) == g for g in groups]
    pad_rows = jnp.zeros((BF16_ROWS - bpt, cols), BF16)

    m_scr[...] = jnp.full(m_scr.shape, NEG, F32)

    def scores(j, g, diagonal):
        k0 = pl.multiple_of(j * tk, tk)
        sb = selt_ref[0, g, pl.ds(pl.multiple_of(j * bpt, bpt), bpt), :]
        og = 1 - g
        q4t_scr[g, og * NSA_HEAD_DIM:og * NSA_HEAD_DIM + BF16_ROWS, :] = jnp.concatenate(
            [head_cols(sb).astype(BF16), pad_rows], axis=0)
        s = _mm(jnp.where(own_lanes[g], ks_ref[0, pl.ds(k0, tk), :], e0t_ref[g]), q4t_scr[g])
        if diagonal:
            kpos = k0 + lax.broadcasted_iota(jnp.int32, (tk, 1), 0)
            s = jnp.where(kpos <= head_cols(t), s, NEG)
        return s

    def absorb(j, g, s):
        k0 = pl.multiple_of(j * tk, tk)
        m_old = m_scr[g]
        m_new = jnp.maximum(m_old, jnp.max(s, axis=0, keepdims=True))
        m_scr[g] = m_new
        p = jnp.exp(s - m_new).astype(BF16)
        acc_scr[g] = (jnp.exp(m_old - m_new) * acc_scr[g]
                      + _mm(with_ones(vs_ref[0, lanes[g], pl.ds(k0, tk)]), p))

    def run(tiles):
        items = [(j, g, d) for j, d in tiles for g in groups]
        ahead = 4
        pending = [scores(*it) for it in items[:ahead]]
        for k, (j, g, _) in enumerate(items):
            s = pending.pop(0)
            absorb(j, g, s)
            if k + ahead < len(items):
                pending.append(scores(*items[k + ahead]))

    n_past = n_kt - 1

    def pair(i, carry):
        run([(2 * i, False), (2 * i + 1, False)])
        return carry

    lax.fori_loop(0, n_past // 2, pair, 0)

    @pl.when(n_past % 2 == 1)
    def _():
        run([(n_past - 1, False), (n_past, True)])

    @pl.when(n_past % 2 == 0)
    def _():
        run([(n_past, True)])

    kw_rows = kw_ref[0, pl.ds(w0, wlen), :]
    for g in groups:
        acc = acc_scr[g]
        o_slc = acc[0:NSA_HEAD_DIM] / acc[NSA_HEAD_DIM:NSA_HEAD_DIM + 1]
        s = _mm(jnp.where(own_lanes[g], kw_rows, jnp.zeros_like(kw_rows)), q4t_scr[g]) + head_cols(wbias)
        p = jnp.exp(s - jnp.max(s, axis=0, keepdims=True)).astype(BF16)
        acc = _mm(with_ones(vw_ref[0, lanes[g], pl.ds(w0, wlen)]), p)
        o_win = acc[0:NSA_HEAD_DIM] / acc[NSA_HEAD_DIM:NSA_HEAD_DIM + 1]

        for r in range(NSA_GROUP):
            h = g * NSA_GROUP + r
            c = GATE_LANE0 + 3 * h
            cs = slice(r * tq, (r + 1) * tq)
            comb_scr[h * NSA_HEAD_DIM:(h + 1) * NSA_HEAD_DIM, :] = (
                o_slc[:, cs] * gate_t[c + 1:c + 2] + o_win[:, cs] * gate_t[c + 2:c + 3])
    o_ref[...] = ocmp_ref[...] + comb_scr[...].T


def _slc_prompt(nq, ocmp, misc, selt, ks, vs, kw, vw, *, batch, seq, tq, tk):
    nt = seq // tq
    nc = seq // SEL_BLOCK
    bpt = tk // SEL_BLOCK
    key_blk = jnp.arange(tk, dtype=jnp.int32)[None, :, None] // SEL_BLOCK
    other0 = (1 - jnp.arange(NSA_KV_HEADS, dtype=jnp.int32))[:, None, None] * NSA_HEAD_DIM
    e0t = (jnp.arange(KV_W, dtype=jnp.int32)[None, None, :] - other0 == key_blk).astype(BF16)
    assert tk % tq == 0 and seq >= WINDOW + tq and bpt <= BF16_ROWS
    cols = NSA_GROUP * tq
    row = lambda n: pl.BlockSpec((tq, n), lambda b, i: (b * nt + i, 0))
    k_rows = pl.BlockSpec((1, seq, KV_W), lambda b, i: (b, 0, 0), pipeline_mode=pl.Buffered(1))
    v_cols = pl.BlockSpec((1, KV_W, seq), lambda b, i: (b, 0, 0), pipeline_mode=pl.Buffered(1))
    return pl.pallas_call(
        functools.partial(_slc_kernel, tq=tq, tk=tk),
        grid=(batch, nt),
        in_specs=[row(NSA_WIDTH), row(NSA_WIDTH), row(LANE),
                  pl.BlockSpec((1, NSA_KV_HEADS, nc, tq), lambda b, i: (b, 0, 0, i)),
                  _const_spec((NSA_KV_HEADS, tk, KV_W)), k_rows, v_cols, k_rows, v_cols],
        out_specs=row(NSA_WIDTH),
        out_shape=jax.ShapeDtypeStruct((batch * seq, NSA_WIDTH), F32),
        scratch_shapes=[pltpu.VMEM((NSA_KV_HEADS, KV_W, cols), BF16),
                        pltpu.VMEM((NSA_KV_HEADS, AUG_ROWS, cols), F32),
                        pltpu.VMEM((NSA_KV_HEADS, 1, cols), F32),
                        pltpu.VMEM((NSA_WIDTH, tq), F32)],
        compiler_params=pltpu.CompilerParams(dimension_semantics=("arbitrary", "arbitrary"),
                                             vmem_limit_bytes=VMEM_LIMIT),
        name="nsa_slc_prompt",
    )(nq, ocmp, misc, selt, e0t, ks.reshape(batch, seq, KV_W), vs, kw.reshape(batch, seq, KV_W), vw)


PAGES_PER_MM = SUBLANE // (PAGE_SIZE // CMP_BLOCK)


def _pool_cmp_kernel(pk_ref, pv_ref, wk_ref, wv_ref, sel_ref, ok_ref, ov_ref, *, pp):
    def quad(i, carry):
        for p_ref, w_ref, o_ref in ((pk_ref, wk_ref, ok_ref), (pv_ref, wv_ref, ov_ref)):
            acc = jnp.zeros((SUBLANE, KV_W), F32)
            for k in range(PAGES_PER_MM):
                y = (p_ref[i * PAGES_PER_MM + k].reshape(KV_W, PAGE_SIZE) * w_ref[...]).astype(BF16)
                acc = acc + _nt(sel_ref[k], y)
            o_ref[pl.ds(pl.multiple_of(i * SUBLANE, SUBLANE), SUBLANE), :] = acc
        return carry

    lax.fori_loop(0, pp // PAGES_PER_MM, quad, 0)


def _pool_compress(pool_k, pool_v, wck_t, wcv_t, *, pp):
    n_pool = pool_k.shape[0]
    bpp = PAGE_SIZE // CMP_BLOCK
    rowid = jnp.arange(SUBLANE, dtype=jnp.int32)[None, :, None]
    want = (bpp * jnp.arange(PAGES_PER_MM, dtype=jnp.int32)[:, None, None]
            + jnp.arange(PAGE_SIZE, dtype=jnp.int32)[None, None, :] // CMP_BLOCK)
    sel = (rowid == want).astype(BF16)
    pg = pl.BlockSpec((pp, NSA_KV_HEADS, NSA_HEAD_DIM, PAGE_SIZE), lambda i: (i, 0, 0, 0))
    og = pl.BlockSpec((pp * bpp, KV_W), lambda i: (i, 0))
    return pl.pallas_call(
        functools.partial(_pool_cmp_kernel, pp=pp),
        grid=(n_pool // pp,),
        in_specs=[pg, pg, _const_spec((KV_W, PAGE_SIZE)), _const_spec((KV_W, PAGE_SIZE)),
                  _const_spec(sel.shape)],
        out_specs=[og, og],
        out_shape=[jax.ShapeDtypeStruct((n_pool * bpp, KV_W), F32)] * 2,
        compiler_params=pltpu.CompilerParams(dimension_semantics=("arbitrary",),
                                             vmem_limit_bytes=VMEM_LIMIT),
        name="pool_compress",
    )(pool_k, pool_v, wck_t, wcv_t, sel)


def _cmp_step_kernel(pt_ref, q_ref, kcn_ref, vcn_ref, gate_ref, wk_ref, wv_ref, pkc_ref, pvc_ref,
                     o_ref, idx_ref, kbuf, vbuf, *, n_pages, ncp, bb):
    b0 = pl.program_id(0) * bb
    bpp = PAGE_SIZE // CMP_BLOCK
    n_past = n_pages * bpp
    t = n_pages * PAGE_SIZE
    blk = lax.broadcasted_iota(jnp.int32, (1, ncp), 1)
    cmask = blk * CMP_BLOCK + (CMP_BLOCK - 1) <= t
    cur = t // SEL_BLOCK
    forced = (blk == 0) | (blk == cur) | (blk == cur - 1)
    scores = []
    for i in range(bb):
        def gather(j, carry):
            p = pt_ref[b0 + i, j]
            for src, dst in ((pkc_ref, kbuf), (pvc_ref, vbuf)):
                for u in range(bpp):
                    dst[i, pl.ds(j * bpp + u, 1), :] = src[pl.ds(p * bpp + u, 1), :]
            return carry

        lax.fori_loop(0, n_pages, gather, 0)
        kbuf[i, n_past:ncp, :] = jnp.zeros((ncp - n_past, KV_W), F32)
        vbuf[i, n_past:ncp, :] = jnp.zeros((ncp - n_past, KV_W), F32)
        kbuf[i, n_past:n_past + 1, :] = kcn_ref[i:i + 1, :] * wk_ref[0:1, :]
        vbuf[i, n_past:n_past + 1, :] = vcn_ref[i:i + 1, :] * wv_ref[0:1, :]
        kc = kbuf[i].astype(BF16)
        vc = vbuf[i].astype(BF16)
        qrow = q_ref[i:i + 1, :]
        for g in range(NSA_KV_HEADS):
            gl = slice(g * NSA_HEAD_DIM, (g + 1) * NSA_HEAD_DIM)
            p = _masked_softmax(_nt(_group_rows(qrow, g), kc[:, gl]), cmask)
            o4 = _mm(p.astype(BF16), vc[:, gl])
            for r in range(NSA_GROUP):
                h = g * NSA_GROUP + r
                c = GATE_LANE0 + 3 * h
                o_ref[i:i + 1, h * NSA_HEAD_DIM:(h + 1) * NSA_HEAD_DIM] = (
                    o4[r:r + 1] * gate_ref[i:i + 1, c:c + 1])
            score = jnp.where(forced, float(NSA_GROUP + 1),
                              jnp.sum(p[0:NSA_GROUP], axis=0, keepdims=True))
            score = jnp.where(blk > cur, -1.0, score)
            scores.append(jnp.where(blk > n_past, -3.0, score))
    score = jnp.concatenate(scores, axis=0)
    lane = lax.broadcasted_iota(jnp.int32, (1, LANE), 1)
    idx = jnp.zeros((bb * NSA_KV_HEADS, LANE), jnp.int32)
    for n in range(TOP_N):
        j = _first_max(score, blk)
        idx = jnp.where(lane == n, j, idx)
        score = jnp.where(blk == j, -4.0, score)
    idx_ref[...] = idx


def _cmp_step(page_table, nq, kcn, vcn, misc, wck, wcv, pkc, pvc, *, bb):
    n, n_pages = page_table.shape
    bpp = PAGE_SIZE // CMP_BLOCK
    ncp = -(-(n_pages * bpp + 1) // LANE) * LANE
    row = lambda w: pl.BlockSpec((bb, w), lambda i, pt: (i, 0))
    cst = lambda shape: pl.BlockSpec(shape, lambda i, pt: (0, 0), pipeline_mode=pl.Buffered(1))
    return pl.pallas_call(
        functools.partial(_cmp_step_kernel, n_pages=n_pages, ncp=ncp, bb=bb),
        grid_spec=pltpu.PrefetchScalarGridSpec(
            num_scalar_prefetch=1, grid=(n // bb,),
            in_specs=[row(NSA_WIDTH), row(KV_W), row(KV_W), row(LANE),
                      cst((CMP_BLOCK, KV_W)), cst((CMP_BLOCK, KV_W)),
                      cst(pkc.shape), cst(pvc.shape)],
            out_specs=[row(NSA_WIDTH), pl.BlockSpec((bb * NSA_KV_HEADS, LANE), lambda i, pt: (i, 0))],
            scratch_shapes=[pltpu.VMEM((bb, ncp, KV_W), F32), pltpu.VMEM((bb, ncp, KV_W), F32)]),
        out_shape=[jax.ShapeDtypeStruct((n, NSA_WIDTH), F32),
                   jax.ShapeDtypeStruct((n * NSA_KV_HEADS, LANE), jnp.int32)],
        compiler_params=pltpu.CompilerParams(dimension_semantics=("arbitrary",),
                                             vmem_limit_bytes=VMEM_LIMIT),
        name="nsa_cmp_step",
    )(page_table, nq.astype(F32), kcn, vcn, misc, wck, wcv, pkc, pvc)


def _slc_step_kernel(pt_ref, idx_ref, q_ref, ocmp_ref, gate_ref, ksn_ref, vsn_ref, kwn_ref, vwn_ref,
                     kwb_ref, vwb_ref, pks_ref, pvs_ref, o_ref, kbuf, vbuf, sem, *, n_pages, wb):
    b = pl.program_id(0)
    bpp = PAGE_SIZE // SEL_BLOCK
    n_past = n_pages * bpp
    t = n_pages * PAGE_SIZE
    n_sel = NSA_KV_HEADS * TOP_N

    def copies(i):
        page = pt_ref[b, jnp.minimum(idx_ref[b, i], n_past - 1) // bpp]
        g = i // TOP_N
        return (pltpu.make_async_copy(pks_ref.at[page, g], kbuf.at[i], sem.at[0, i]),
                pltpu.make_async_copy(pvs_ref.at[page, g], vbuf.at[i], sem.at[1, i]))

    for i in range(n_sel):
        for cp in copies(i):
            cp.start()
    for i in range(n_sel):
        for cp in copies(i):
            cp.wait()

    kidx = lax.broadcasted_iota(jnp.int32, (1, TOP_N * PAGE_SIZE), 1)
    krow = kidx % PAGE_SIZE
    wpos = t - wb + lax.broadcasted_iota(jnp.int32, (1, wb), 1)
    wmask = (wpos >= 0) & (wpos <= t) & (t - wpos < WINDOW)
    qrow = q_ref[0].astype(F32)

    def with_new_key(s_old, mask_old, vt_old, k_new, v_new, q4, use_new):
        sn = jnp.sum(q4.astype(F32) * k_new.astype(BF16).astype(F32), axis=-1, keepdims=True)
        sn = jnp.where(use_new, sn, NEG)
        so = jnp.where(mask_old, s_old, NEG)
        m = jnp.maximum(jnp.max(so, axis=-1, keepdims=True), sn)
        eo = jnp.where(mask_old, jnp.exp(so - m), 0.0)
        en = jnp.where(use_new, jnp.exp(sn - m), 0.0)
        den = jnp.maximum(jnp.sum(eo, axis=-1, keepdims=True) + en, 1e-30)
        return (_nt((eo / den).astype(BF16), vt_old)
                + (en / den).astype(BF16).astype(F32) * v_new.astype(BF16).astype(F32))

    for g in range(NSA_KV_HEADS):
        gl = slice(g * NSA_HEAD_DIM, (g + 1) * NSA_HEAD_DIM)
        q4 = _group_rows(qrow, g)
        kblk = jnp.zeros((1, TOP_N * PAGE_SIZE), jnp.int32)
        has_new = False
        for n in range(TOP_N):
            blk = idx_ref[b, g * TOP_N + n]
            has_new = jnp.logical_or(has_new, blk >= n_past)
            kblk = jnp.where(kidx // PAGE_SIZE == n, blk, kblk)
        valid = ((krow // SEL_BLOCK == kblk % bpp) & (kblk < n_past)
                 & ((kblk // bpp) * PAGE_SIZE + krow <= t))
        kt = jnp.concatenate([kbuf[g * TOP_N + n] for n in range(TOP_N)], axis=1).astype(BF16)
        vt = jnp.concatenate([vbuf[g * TOP_N + n] for n in range(TOP_N)], axis=1).astype(BF16)
        o_slc = with_new_key(_mm(q4, kt), valid, vt, ksn_ref[0, :, gl], vsn_ref[0, :, gl], q4, has_new)
        o_win = with_new_key(_mm(q4, kwb_ref[0, g].astype(BF16)), wmask, vwb_ref[0, g].astype(BF16),
                             kwn_ref[0, :, gl], vwn_ref[0, :, gl], q4, True)
        for r in range(NSA_GROUP):
            h = g * NSA_GROUP + r
            hl = slice(h * NSA_HEAD_DIM, (h + 1) * NSA_HEAD_DIM)
            c = GATE_LANE0 + 3 * h
            o_ref[0, :, hl] = (ocmp_ref[0, :, hl] + o_slc[r:r + 1] * gate_ref[0, :, c + 1:c + 2]
                               + o_win[r:r + 1] * gate_ref[0, :, c + 2:c + 3])


def _slc_step(page_table, idx, nq, ocmp, misc, ksn, vsn, kwn, vwn, buf_kw, buf_vw, pool_ks, pool_vs):
    n, n_pages = page_table.shape
    wb = buf_kw.shape[-1]
    n_sel = NSA_KV_HEADS * TOP_N
    r3 = lambda a: a.reshape(n, 1, a.shape[-1])
    one = lambda w: pl.BlockSpec((1, 1, w), lambda i, pt, ix: (i, 0, 0))
    win = pl.BlockSpec((1, NSA_KV_HEADS, NSA_HEAD_DIM, wb), lambda i, pt, ix: (i, 0, 0, 0))
    hbm = pl.BlockSpec(memory_space=pl.ANY)
    return pl.pallas_call(
        functools.partial(_slc_step_kernel, n_pages=n_pages, wb=wb),
        grid_spec=pltpu.PrefetchScalarGridSpec(
            num_scalar_prefetch=2, grid=(n,),
            in_specs=[one(NSA_WIDTH), one(NSA_WIDTH), one(LANE), one(KV_W), one(KV_W), one(KV_W),
                      one(KV_W), win, win, hbm, hbm],
            out_specs=one(NSA_WIDTH),
            scratch_shapes=[pltpu.VMEM((n_sel, NSA_HEAD_DIM, PAGE_SIZE), F32),
                            pltpu.VMEM((n_sel, NSA_HEAD_DIM, PAGE_SIZE), F32),
                            pltpu.SemaphoreType.DMA((2, n_sel))]),
        out_shape=jax.ShapeDtypeStruct((n, 1, NSA_WIDTH), F32),
        compiler_params=pltpu.CompilerParams(dimension_semantics=("arbitrary",),
                                             vmem_limit_bytes=VMEM_LIMIT),
        name="nsa_slc_step",
    )(page_table, idx, r3(nq), r3(ocmp), r3(misc), r3(ksn), r3(vsn), r3(kwn), r3(vwn),
      buf_kw, buf_vw, pool_ks, pool_vs)


def _post_kernel(x_ref, og_ref, on_ref, pe_ref, p0_ref, p1_ref, wo_ref, gf_ref, wup_ref, wcv_ref,
                 bcv_ref, wdn_ref, gp_ref, wple_ref, wpg_ref, gfin_ref, y_ref, ut_ref,
                 carry, ext, act_scr, *, tm, tail, shift):
    attn = (_mm(og_ref[...].astype(BF16), wo_ref[0:GLA_WIDTH, :])
            + _mm(on_ref[...].astype(BF16), wo_ref[GLA_WIDTH:, :]))
    h = x_ref[...] + attn
    hn = _rms(h, gf_ref[...]).astype(BF16)
    if shift:
        @pl.when(pl.program_id(1) == 0)
        def _():
            carry[...] = jnp.zeros_like(carry)

    for j in range(N_FF_CHUNKS):
        u = _mm(hn, wup_ref[j])
        wc = wcv_ref[j]
        if shift:
            e = j % 2
            ext[e, 0:SUBLANE, :] = carry[j]
            ext[e, SUBLANE:, :] = u
            um2 = ext[e, pl.ds(SUBLANE - 2, tm), :]
            um1 = ext[e, pl.ds(SUBLANE - 1, tm), :]
            carry[j] = u[tm - SUBLANE:, :]
        else:
            um2 = p0_ref[j]
            um1 = p1_ref[j]
        c = bcv_ref[j] + um2 * wc[0:1] + um1 * wc[1:2] + u * wc[2:3]
        ut_ref[0, j] = u[tm - tail:, :]
        a = c[:, :FF_CHUNK]
        act_scr[:, j * FF_CHUNK:(j + 1) * FF_CHUNK] = (a * _sigmoid(a) * c[:, FF_CHUNK:]).astype(BF16)
    h = h + _mm(act_scr[...], wdn_ref[...])
    gate = _sigmoid(_mm(_rms(h, gp_ref[...]).astype(BF16), wpg_ref[...]))
    h = h + _mm(pe_ref[...].astype(BF16), wple_ref[...]) * gate
    y_ref[...] = _rms(h, gfin_ref[...])


def _post(x2d, ogla, onsa, pe2d, prev, pw, g_final, *, batch, tm, shift):
    m = x2d.shape[0]
    nt = m // (batch * tm)
    tail = SUBLANE if shift else tm
    row = lambda n: pl.BlockSpec((tm, n), lambda b, i: (b * nt + i, 0))
    cst = lambda shape: pl.BlockSpec(shape, lambda b, i: (0,) * len(shape),
                                     pipeline_mode=pl.Buffered(1))
    if shift:
        prev_spec = cst((1, 1, LANE))
        p0 = p1 = jnp.zeros((1, 1, LANE), F32)
    else:
        prev_spec = pl.BlockSpec((N_FF_CHUNKS, tm, 2 * FF_CHUNK), lambda b, i: (0, b * nt + i, 0))
        p0, p1 = prev
    return pl.pallas_call(
        functools.partial(_post_kernel, tm=tm, tail=tail, shift=shift),
        grid=(batch, nt),
        in_specs=[row(D_MODEL), row(GLA_WIDTH), row(NSA_WIDTH), row(PLE_DIM), prev_spec, prev_spec,
                  cst((D_MODEL, D_MODEL)), cst((1, D_MODEL)),
                  cst((N_FF_CHUNKS, D_MODEL, 2 * FF_CHUNK)), cst((N_FF_CHUNKS, CONV_W, 2 * FF_CHUNK)),
                  cst((N_FF_CHUNKS, 1, 2 * FF_CHUNK)), cst((D_FF, D_MODEL)),
                  cst((1, D_MODEL)), cst((PLE_DIM, D_MODEL)), cst((D_MODEL, D_MODEL)),
                  cst((1, D_MODEL))],
        out_specs=[row(D_MODEL),
                   pl.BlockSpec((1, N_FF_CHUNKS, tail, 2 * FF_CHUNK), lambda b, i: (b, 0, 0, 0))],
        out_shape=[jax.ShapeDtypeStruct((m, D_MODEL), F32),
                   jax.ShapeDtypeStruct((batch, N_FF_CHUNKS, tail, 2 * FF_CHUNK), F32)],
        scratch_shapes=[pltpu.VMEM((N_FF_CHUNKS, SUBLANE, 2 * FF_CHUNK), F32),
                        pltpu.VMEM((2, tm + SUBLANE, 2 * FF_CHUNK), F32),
                        pltpu.VMEM((tm, D_FF), BF16)],
        compiler_params=pltpu.CompilerParams(dimension_semantics=("arbitrary", "arbitrary"),
                                             vmem_limit_bytes=VMEM_LIMIT),
        name="post",
    )(x2d, ogla, onsa, pe2d, p0, p1, pw["w_o"], pw["g_ffn"], pw["w_up"], pw["w_conv"], pw["b_conv"],
      pw["w_down"], pw["g_ple"], pw["w_ple"], pw["w_ple_gate"], g_final)


def _ff_chunked(a):
    lead = a.shape[:-1]
    a = a.reshape(*lead, 2, N_FF_CHUNKS, FF_CHUNK)
    a = jnp.moveaxis(a, -2, 0)
    return a.reshape(N_FF_CHUNKS, *lead, 2 * FF_CHUNK)


def _ff_unchunked(a):
    r = a.shape[1]
    return a.reshape(N_FF_CHUNKS, r, 2, FF_CHUNK).transpose(1, 2, 0, 3).reshape(r, 2 * D_FF)


def _pack_weights(i, g_attn, w_in, w_gla_gate, b_gla_gate, g_gla_out, b_nsa_gate, w_cmp_k, w_cmp_v,
                  w_o, g_ffn, w_up, w_conv, b_conv, w_down, g_ple, w_ple, w_ple_gate):
    w = w_in[i]
    o_glr = 2 * GLA_QK + 2 * GLA_WIDTH
    o_nq = o_glr + GLA_GATE_RANK
    o_ng = o_nq + NSA_WIDTH + 6 * KV_W
    pad = jnp.zeros((D_MODEL, LANE - GLA_GATE_RANK - 3 * NSA_HEADS), w.dtype)
    w_pack = jnp.concatenate([w[:, :o_glr], w[:, o_nq:o_ng], w[:, o_glr:o_nq], w[:, o_ng:], pad], axis=1)
    w_gate = jnp.zeros((LANE, GLA_QK), F32).at[:GLA_GATE_RANK].set(w_gla_gate[i])
    b_misc = jnp.zeros((1, LANE), F32).at[0, GATE_LANE0:GATE_LANE0 + 3 * NSA_HEADS].set(b_nsa_gate[i])
    return {
        "g_attn": g_attn[i][None], "w_in": w_pack.astype(BF16), "w_gate": w_gate.astype(BF16),
        "b_gate": b_gla_gate[i][None], "b_misc": b_misc, "g_gla_out": g_gla_out[i][None],
        "w_ck": jnp.repeat(w_cmp_k[i], NSA_HEAD_DIM, axis=1),
        "w_cv": jnp.repeat(w_cmp_v[i], NSA_HEAD_DIM, axis=1),
        "w_ck_t": jnp.tile(jnp.repeat(w_cmp_k[i].T, NSA_HEAD_DIM, axis=0), (1, PAGE_SIZE // CMP_BLOCK)),
        "w_cv_t": jnp.tile(jnp.repeat(w_cmp_v[i].T, NSA_HEAD_DIM, axis=0), (1, PAGE_SIZE // CMP_BLOCK)),
        "w_o": w_o[i].astype(BF16), "g_ffn": g_ffn[i][None],
        "w_up": _ff_chunked(w_up[i]).astype(BF16), "w_conv": _ff_chunked(w_conv[i]),
        "b_conv": _ff_chunked(b_conv[i][None]),
        "w_down": w_down[i].astype(BF16),
        "g_ple": g_ple[i][None], "w_ple": w_ple[i].astype(BF16),
        "w_ple_gate": w_ple_gate[i].astype(BF16),
    }


def _rope_tables(pos):
    half = ROT_DIM // 2
    inv = ROPE_THETA ** (-jnp.arange(half, dtype=F32) / half)
    ang = pos.astype(F32)[:, None] * inv[None, :]
    d = jnp.arange(LANE) % NSA_HEAD_DIM
    cos = jnp.cos(ang)[:, d % half]
    sin = jnp.sin(ang)[:, d % half]
    return (jnp.where(d < ROT_DIM, cos, 1.0), jnp.where(d < half, -sin, 0.0),
            jnp.where((d >= half) & (d < ROT_DIM), sin, 0.0))


def _kv4(a, b, t):
    return a.reshape(b, t, NSA_KV_HEADS, NSA_HEAD_DIM)


def _prompt_layer(x, pe, pw, g_final):
    b, t, _ = x.shape
    m = b * t
    tabs = _rope_tables(jnp.arange(t, dtype=jnp.int32))
    (gqk, gv, gr, la, nq, misc, kc, vc, ks, vs, kw, vw, ks16, vs16, kw16, vw16, kcc, vcc) = _inproj(
        x.reshape(m, D_MODEL), tabs, pw, tm=512, batch=b)
    ogla, h_gla = _gla_prompt(gqk, gv, gr, la, pw["g_gla_out"], batch=b, seq=t, tg=512)
    ocmp, selt = _cmp_prompt(nq, kcc, vcc, misc, batch=b, seq=t, tq=256)
    onsa = _slc_prompt(nq, ocmp, misc, selt, ks16, vs16, kw16, vw16, batch=b, seq=t, tq=128, tk=512)
    y, utail = _post(x.reshape(m, D_MODEL), ogla, onsa, pe.reshape(m, PLE_DIM), None, pw, g_final,
                     batch=b, tm=512, shift=True)
    conv_new = jax.vmap(_ff_unchunked)(utail)[:, SUBLANE - (CONV_W - 1):]
    nw = min(WINDOW, t)

    def rows(a):
        return a.reshape(b, NSA_KV_HEADS, NSA_HEAD_DIM, t).transpose(0, 3, 1, 2)

    state = (rows(kc), rows(vc), rows(ks), rows(vs), rows(kw)[:, t - nw:], rows(vw)[:, t - nw:],
             h_gla, conv_new)
    return y.reshape(b, t, D_MODEL), state


def _sample_layer(x, pe, pw, g_final, pools, bufs, h0, conv_prev, page_table):
    n, t, _ = x.shape
    assert t == 1, "the sample group is written for one new token per request"
    n_pages = page_table.shape[1]
    pos = jnp.full((n,), n_pages * PAGE_SIZE, jnp.int32)
    tabs = _rope_tables(pos)
    x2d = x.reshape(n, D_MODEL)
    (gqk, gv, gr, la, nq, misc, kc, vc, ks, vs, kw, vw) = _inproj(x2d, tabs, pw, tm=n)
    ogla, h_gla = _gla_step(gqk, gv, gr, la, pw["g_gla_out"], h0, bb=SUBLANE)
    pool_kc, pool_vc, pool_ks, pool_vs = [p.transpose(0, 2, 3, 1) for p in pools]
    n_pool = pool_kc.shape[0]
    pp = max(p for p in range(PAGES_PER_MM, 65, PAGES_PER_MM) if n_pool % p == 0)
    pkc, pvc = _pool_compress(pool_kc, pool_vc, pw["w_ck_t"], pw["w_cv_t"], pp=pp)
    ocmp, idx = _cmp_step(page_table, nq, kc, vc, misc, pw["w_ck"], pw["w_cv"], pkc, pvc, bb=SUBLANE)
    idx = idx.reshape(n, NSA_KV_HEADS, LANE)[:, :, :TOP_N].reshape(n, NSA_KV_HEADS * TOP_N)
    buf_kw, buf_vw = bufs
    onsa = _slc_step(page_table, idx, nq, ocmp, misc, ks, vs, kw, vw,
                     buf_kw.transpose(0, 2, 3, 1), buf_vw.transpose(0, 2, 3, 1), pool_ks, pool_vs)
    prev = (_ff_chunked(conv_prev[:, 0]), _ff_chunked(conv_prev[:, 1]))
    y, utail = _post(x2d, ogla, onsa.reshape(n, NSA_WIDTH), pe.reshape(n, PLE_DIM), prev, pw, g_final,
                     batch=1, tm=n, shift=False)
    u = _ff_unchunked(utail[0])
    conv_new = jnp.stack([conv_prev[:, 1], u], axis=1)
    wb = buf_kw.shape[1]
    nw = min(WINDOW, wb + 1)
    kwin = jnp.concatenate([buf_kw, _kv4(kw, n, 1)], axis=1)[:, wb + 1 - nw:]
    vwin = jnp.concatenate([buf_vw, _kv4(vw, n, 1)], axis=1)[:, wb + 1 - nw:]
    state = (_kv4(kc, n, 1), _kv4(vc, n, 1), _kv4(ks, n, 1), _kv4(vs, n, 1), kwin, vwin, h_gla,
             conv_new)
    return y.reshape(n, 1, D_MODEL), state


def kernel(x_prompt, x_sample, p_prompt, p_sample, cache_k_cmp, cache_v_cmp, cache_k_slc, cache_v_slc, cache_k_win, cache_v_win, state_gla, state_conv, page_table, g_attn, w_in, w_gla_gate, b_gla_gate, g_gla_out, b_nsa_gate, w_cmp_k, w_cmp_v, w_o, g_ffn, w_up, w_conv, b_conv, w_down, g_ple, w_ple, w_ple_gate, g_final):
    depth = w_in.shape[0]
    assert depth == 1, "the final norm is fused into the layer kernel; written for a one-layer trunk"
    pw = _pack_weights(0, g_attn, w_in, w_gla_gate, b_gla_gate, g_gla_out, b_nsa_gate, w_cmp_k,
                       w_cmp_v, w_o, g_ffn, w_up, w_conv, b_conv, w_down, g_ple, w_ple, w_ple_gate)
    gfin = g_final[None]
    y_p, st_p = _prompt_layer(x_prompt, p_prompt[0], pw, gfin)
    y_s, st_s = _sample_layer(
        x_sample, p_sample[0], pw, gfin,
        (cache_k_cmp[0], cache_v_cmp[0], cache_k_slc[0], cache_v_slc[0]),
        (cache_k_win[0], cache_v_win[0]), state_gla[0], state_conv[0], page_table)
    return (y_p, y_s, *[a[None] for a in st_p], *[a[None] for a in st_s])
```

```python
import functools

import jax
import jax.numpy as jnp
from jax import lax
from jax.experimental import pallas as pl
from jax.experimental.pallas import tpu as pltpu

F32 = jnp.float32
BF16 = jnp.bfloat16

D_MODEL = 1024
PAGE_SIZE = 128
GLA_WIDTH = 512
GLA_HEADS = 4
GLA_QK = 256
GLA_DK = 64
GLA_DV = 128
GLA_GATE_RANK = 16
GLA_TAU = 16.0
GLA_CHUNK = 64
NSA_WIDTH = 512
NSA_HEADS = 8
NSA_HEAD_DIM = 64
NSA_KV_HEADS = 2
NSA_GROUP = 4
KV_W = 128
CMP_BLOCK = 64
SEL_BLOCK = 64
TOP_N = 16
N_FORCED = 3
WINDOW = 512
ROT_DIM = 16
ROPE_THETA = 500000.0
D_FF = 2816
CONV_W = 3
PLE_DIM = 256
EPS = 1e-6

LANE = 128
SUBLANE = 8
NEG = -1e30
FF_CHUNK = 256
N_FF_CHUNKS = D_FF // FF_CHUNK
VMEM_LIMIT = 56 * 1024 * 1024

C_GQ, C_GK, C_GV, C_GR, C_NQ = 0, 256, 512, 1024, 1536
C_KC, C_VC, C_KS, C_VS, C_KW, C_VW, C_MISC = 2048, 2176, 2304, 2432, 2560, 2688, 2816
D_PACK = 2944
GATE_LANE0 = GLA_GATE_RANK


def _nt(a, b):
    return lax.dot_general(a, b, (((1,), (1,)), ((), ())), preferred_element_type=F32)


def _tn(a, b):
    return lax.dot_general(a, b, (((0,), (0,)), ((), ())), preferred_element_type=F32)


def _mm(a, b):
    return jnp.dot(a, b, preferred_element_type=F32)


def _rms(x, g):
    return x * lax.rsqrt(jnp.mean(x * x, axis=-1, keepdims=True) + EPS) * g


def _sigmoid(x):
    return 1.0 / (1.0 + jnp.exp(-x))


def _group_rows(qrow, g):
    parts = [qrow[:, (g * NSA_GROUP + r) * NSA_HEAD_DIM:(g * NSA_GROUP + r + 1) * NSA_HEAD_DIM]
             for r in range(NSA_GROUP)]
    parts.append(jnp.zeros((SUBLANE - NSA_GROUP, NSA_HEAD_DIM), F32))
    return jnp.concatenate(parts, axis=0).astype(BF16)


def _const_spec(shape):
    nd = len(shape)
    return pl.BlockSpec(shape, lambda *_: (0,) * nd, pipeline_mode=pl.Buffered(1))


def _inproj_kernel(x_ref, g_ref, w_ref, wgate_ref, bgate_ref, bmisc_ref, cos_ref, sa_ref, sb_ref,
                   wck_ref, wcv_ref, *out_refs, prompt):
    gqk_ref, gv_ref, gr_ref, la_ref, nq_ref, misc_ref = out_refs[:6]
    kv_refs = out_refs[6:12]
    xn = _rms(x_ref[...], g_ref[...]).astype(BF16)

    def proj(lo, hi):
        return _mm(xn, w_ref[:, lo:hi])

    cos, sa, sb = cos_ref[...], sa_ref[...], sb_ref[...]

    def rope(z):
        return z * cos + pltpu.roll(z, LANE - ROT_DIM // 2, 1) * sa + pltpu.roll(z, ROT_DIM // 2, 1) * sb

    gqk_ref[...] = proj(C_GQ, C_GV)
    gv_ref[...] = proj(C_GV, C_GR)
    gr_ref[...] = proj(C_GR, C_NQ)
    for j in range(NSA_WIDTH // LANE):
        z = proj(C_NQ + j * LANE, C_NQ + (j + 1) * LANE)
        nq_ref[:, j * LANE:(j + 1) * LANE] = (rope(z) * (NSA_HEAD_DIM ** -0.5)).astype(BF16)
    kv = [rope(proj(C_KC, C_VC)), proj(C_VC, C_KS), rope(proj(C_KS, C_VS)), proj(C_VS, C_KW),
          rope(proj(C_KW, C_VW)), proj(C_VW, C_MISC)]
    z = proj(C_MISC, D_PACK)
    misc_ref[...] = _sigmoid(z + bmisc_ref[...])
    pre = _mm(z.astype(BF16), wgate_ref[...]) + bgate_ref[...]
    la_ref[...] = (jnp.minimum(pre, 0.0) - jnp.log(1.0 + jnp.exp(-jnp.abs(pre)))) * (1.0 / GLA_TAU)
    if not prompt:
        for ref, a in zip(kv_refs, kv):
            ref[...] = a
        return
    kv_t = [a.T for a in kv]
    for ref, a in zip(kv_refs, kv_t):
        ref[0] = a
    ks16_ref, vs16_ref, kw16_ref, vw16_ref, kcc_ref, vcc_ref = out_refs[12:]
    ks16_ref[...] = kv[2].astype(BF16)
    kw16_ref[...] = kv[4].astype(BF16)
    vs16_ref[0] = kv_t[3].astype(BF16)
    vw16_ref[0] = kv_t[5].astype(BF16)
    nb = kv[0].shape[0] // CMP_BLOCK
    kcc_ref[...] = jnp.sum(kv[0].reshape(nb, CMP_BLOCK, KV_W) * wck_ref[...][None], axis=1)
    vcc_ref[...] = jnp.sum(kv[1].reshape(nb, CMP_BLOCK, KV_W) * wcv_ref[...][None], axis=1)


def _inproj(x2d, tabs, pw, *, tm, batch=None):
    m = x2d.shape[0]
    prompt = batch is not None
    n_tab = tabs[0].shape[0] // tm
    row = lambda n: pl.BlockSpec((tm, n), lambda i: (i, 0))
    tab = pl.BlockSpec((tm, LANE), lambda i: (i % n_tab, 0))
    widths = [2 * GLA_QK, GLA_WIDTH, GLA_WIDTH, GLA_QK, NSA_WIDTH, LANE]
    dtypes = [F32, F32, F32, F32, BF16, F32]
    out_shape = [jax.ShapeDtypeStruct((m, n), d) for n, d in zip(widths, dtypes)]
    out_specs = [row(n) for n in widths]
    if prompt:
        seq = m // batch
        nt = seq // tm
        kvt = pl.BlockSpec((1, KV_W, tm), lambda i: (i // nt, 0, i % nt))
        nb = tm // CMP_BLOCK
        rows16 = jax.ShapeDtypeStruct((m, KV_W), BF16)
        cols16 = jax.ShapeDtypeStruct((batch, KV_W, seq), BF16)
        out_shape += ([jax.ShapeDtypeStruct((batch, KV_W, seq), F32)] * 6
                      + [rows16, cols16, rows16, cols16]
                      + [jax.ShapeDtypeStruct((m // CMP_BLOCK, KV_W), F32)] * 2)
        out_specs += ([kvt] * 6 + [row(KV_W), kvt, row(KV_W), kvt]
                      + [pl.BlockSpec((nb, KV_W), lambda i: (i, 0))] * 2)
    else:
        out_shape += [jax.ShapeDtypeStruct((m, KV_W), F32)] * 6
        out_specs += [row(KV_W)] * 6
    return pl.pallas_call(
        functools.partial(_inproj_kernel, prompt=prompt),
        grid=(m // tm,),
        in_specs=[row(D_MODEL), _const_spec((1, D_MODEL)), _const_spec((D_MODEL, D_PACK)),
                  _const_spec((LANE, GLA_QK)), _const_spec((1, GLA_QK)), _const_spec((1, LANE)),
                  tab, tab, tab, _const_spec((CMP_BLOCK, KV_W)), _const_spec((CMP_BLOCK, KV_W))],
        out_specs=out_specs, out_shape=out_shape,
        compiler_params=pltpu.CompilerParams(dimension_semantics=("arbitrary",),
                                             vmem_limit_bytes=VMEM_LIMIT),
        name="inproj",
    )(x2d, pw["g_attn"], pw["w_in"], pw["w_gate"], pw["b_gate"], pw["b_misc"], *tabs,
      pw["w_ck"], pw["w_cv"])


def _gla_kernel(qk_ref, v_ref, r_ref, la_ref, g_ref, o_ref, hout_ref, ht_scr, o_scr, *, n_chunks):
    t = pl.program_id(1)

    @pl.when(t == 0)
    def _():
        ht_scr[...] = jnp.zeros_like(ht_scr)

    ri = lax.broadcasted_iota(jnp.int32, (GLA_CHUNK, GLA_CHUNK), 0)
    ci = lax.broadcasted_iota(jnp.int32, (GLA_CHUNK, GLA_CHUNK), 1)
    causal = ri >= ci
    tril = causal.astype(F32)

    def chunk(c, carry):
        r0 = pl.multiple_of(c * GLA_CHUNK, GLA_CHUNK)
        rows = pl.ds(r0, GLA_CHUNK)
        bc = jnp.dot(tril, la_ref[rows, :], preferred_element_type=F32,
                     precision=lax.Precision.HIGHEST)
        blast = bc[GLA_CHUNK - 1:GLA_CHUNK, :]
        q = qk_ref[rows, 0:GLA_QK]
        k = qk_ref[rows, GLA_QK:2 * GLA_QK]
        qe = (q * jnp.exp(bc) * (GLA_DK ** -0.5)).astype(BF16)
        ke = (k * jnp.exp(-bc)).astype(BF16)
        kd = (k * jnp.exp(blast - bc)).astype(BF16)
        decay = jnp.exp(blast)
        for h in range(GLA_HEADS):
            ks = slice(h * GLA_DK, (h + 1) * GLA_DK)
            vh = v_ref[rows, h * GLA_DV:(h + 1) * GLA_DV].astype(BF16)
            ht = ht_scr[h]
            att = jnp.where(causal, _nt(qe[:, ks], ke[:, ks]), 0.0).astype(BF16)
            o_scr[rows, h * GLA_DV:(h + 1) * GLA_DV] = _mm(att, vh) + _nt(qe[:, ks], ht.astype(BF16))
            ht_scr[h] = ht * decay[:, ks] + _tn(vh, kd[:, ks])
        return carry

    lax.fori_loop(0, n_chunks, chunk, 0, unroll=4)
    for h in range(GLA_HEADS):
        hs = slice(h * GLA_DV, (h + 1) * GLA_DV)
        r = r_ref[:, hs]
        o_ref[:, hs] = _rms(o_scr[:, hs], g_ref[:, hs]) * (r * _sigmoid(r))

    @pl.when(t == pl.num_programs(1) - 1)
    def _():
        for h in range(GLA_HEADS):
            hout_ref[0, h] = ht_scr[h].T


def _gla_prompt(gqk, gv, gr, la, g_out, *, batch, seq, tg):
    nt = seq // tg
    row = lambda n: pl.BlockSpec((tg, n), lambda b, t: (b * nt + t, 0))
    return pl.pallas_call(
        functools.partial(_gla_kernel, n_chunks=tg // GLA_CHUNK),
        grid=(batch, nt),
        in_specs=[row(2 * GLA_QK), row(GLA_WIDTH), row(GLA_WIDTH), row(GLA_QK),
                  pl.BlockSpec((1, GLA_WIDTH), lambda b, t: (0, 0))],
        out_specs=[row(GLA_WIDTH),
                   pl.BlockSpec((1, GLA_HEADS, GLA_DK, GLA_DV), lambda b, t: (b, 0, 0, 0))],
        out_shape=[jax.ShapeDtypeStruct((batch * seq, GLA_WIDTH), F32),
                   jax.ShapeDtypeStruct((batch, GLA_HEADS, GLA_DK, GLA_DV), F32)],
        scratch_shapes=[pltpu.VMEM((GLA_HEADS, GLA_DV, GLA_DK), F32),
                        pltpu.VMEM((tg, GLA_WIDTH), F32)],
        compiler_params=pltpu.CompilerParams(dimension_semantics=("arbitrary", "arbitrary"),
                                             vmem_limit_bytes=VMEM_LIMIT),
        name="gla_prompt",
    )(gqk, gv, gr, la, g_out)


def _gla_step_kernel(qt_ref, kt_ref, at_ref, v_ref, r_ref, g_ref, h0_ref, o_ref, h_ref, *, bb):
    for b in range(bb):
        for h in range(GLA_HEADS):
            ds_ = slice(h * GLA_DK, (h + 1) * GLA_DK)
            vs_ = slice(h * GLA_DV, (h + 1) * GLA_DV)
            qcol = qt_ref[0, ds_, b:b + 1] * (GLA_DK ** -0.5)
            kcol = kt_ref[0, ds_, b:b + 1]
            acol = jnp.exp(at_ref[0, ds_, b:b + 1])
            hn = acol * h0_ref[b, h] + kcol * v_ref[b:b + 1, vs_]
            h_ref[b, h] = hn
            o = jnp.sum(qcol * hn, axis=0, keepdims=True)
            r = r_ref[b:b + 1, vs_]
            o_ref[b:b + 1, vs_] = _rms(o, g_ref[:, vs_]) * (r * _sigmoid(r))


def _gla_step(gqk, gv, gr, la, g_out, h0, *, bb):
    n = gqk.shape[0]
    ns = n // bb

    def cols(a):
        return a.reshape(ns, bb, GLA_QK).transpose(0, 2, 1)

    colspec = pl.BlockSpec((1, GLA_QK, bb), lambda i: (i, 0, 0))
    row = pl.BlockSpec((bb, GLA_WIDTH), lambda i: (i, 0))
    st = pl.BlockSpec((bb, GLA_HEADS, GLA_DK, GLA_DV), lambda i: (i, 0, 0, 0))
    return pl.pallas_call(
        functools.partial(_gla_step_kernel, bb=bb),
        grid=(ns,),
        in_specs=[colspec, colspec, colspec, row, row,
                  pl.BlockSpec((1, GLA_WIDTH), lambda i: (0, 0)), st],
        out_specs=[row, st],
        out_shape=[jax.ShapeDtypeStruct((n, GLA_WIDTH), F32),
                   jax.ShapeDtypeStruct(h0.shape, F32)],
        compiler_params=pltpu.CompilerParams(dimension_semantics=("arbitrary",)),
        name="gla_step",
    )(cols(gqk[:, :GLA_QK]), cols(gqk[:, GLA_QK:]), cols(la), gv, gr, g_out, h0)


def _masked_softmax(s, mask):
    s = jnp.where(mask, s, NEG)
    m = jnp.max(s, axis=-1, keepdims=True)
    e = jnp.where(mask, jnp.exp(s - m), 0.0)
    return e / jnp.maximum(jnp.sum(e, axis=-1, keepdims=True), 1e-30)


def _first_max(score, blk):
    m = jnp.max(score, axis=-1, keepdims=True)
    return jnp.min(jnp.where(score == m, blk, score.shape[-1]), axis=-1, keepdims=True)


def _top_blocks(score, blk, n_pick):
    sel = jnp.zeros(score.shape, jnp.bool_)
    for _ in range(n_pick):
        pick = blk == _first_max(score, blk)
        sel = jnp.logical_or(sel, pick)
        score = jnp.where(pick, -2.0, score)
    return sel


def _cmp_kernel(q_ref, kc_ref, vc_ref, gate_ref, o_ref, selt_ref, *, tq, nc):
    q0 = pl.program_id(1) * tq
    t = q0 + lax.broadcasted_iota(jnp.int32, (tq, 1), 0)

    def run(ncx):
        blk = lax.broadcasted_iota(jnp.int32, (1, ncx), 1)
        cmask = blk * CMP_BLOCK + (CMP_BLOCK - 1) <= t
        cur = t // SEL_BLOCK
        forced = (blk == 0) | (blk == cur) | (blk == cur - 1)
        future = blk > cur
        kc = kc_ref[0, 0:ncx, :].astype(BF16)
        vc = vc_ref[0, 0:ncx, :].astype(BF16)
        scores = []
        for g in range(NSA_KV_HEADS):
            gl = slice(g * NSA_HEAD_DIM, (g + 1) * NSA_HEAD_DIM)
            psum = jnp.zeros((tq, ncx), F32)
            for r in range(NSA_GROUP):
                h = g * NSA_GROUP + r
                hl = slice(h * NSA_HEAD_DIM, (h + 1) * NSA_HEAD_DIM)
                p = _masked_softmax(_nt(q_ref[:, hl], kc[:, gl]), cmask)
                psum = psum + p
                c = GATE_LANE0 + 3 * h
                o_ref[:, hl] = _mm(p.astype(BF16), vc[:, gl]) * gate_ref[:, c:c + 1]
            scores.append(jnp.where(future, -1.0, jnp.where(forced, -2.0, psum)))
        sel = _top_blocks(jnp.concatenate(scores, axis=0), blk, TOP_N - N_FORCED)
        for g in range(NSA_KV_HEADS):
            keep = jnp.logical_and(jnp.logical_or(sel[g * tq:(g + 1) * tq], forced),
                                   jnp.logical_not(future))
            selt_ref[0, g, 0:ncx, :] = jnp.where(keep, 0.0, NEG).T
            if ncx < nc:
                selt_ref[0, g, ncx:nc, :] = jnp.full((nc - ncx, tq), NEG, F32)

    half = nc // 2
    if half >= TOP_N:
        early = q0 + tq <= half * CMP_BLOCK
        pl.when(early)(lambda: run(half))
        pl.when(jnp.logical_not(early))(lambda: run(nc))
    else:
        run(nc)


def _cmp_prompt(nq, kcc, vcc, misc, *, batch, seq, tq):
    nc = seq // CMP_BLOCK
    assert nc >= TOP_N
    nt = seq // tq
    row = lambda n: pl.BlockSpec((tq, n), lambda b, i: (b * nt + i, 0))
    cblk = pl.BlockSpec((1, nc, KV_W), lambda b, i: (b, 0, 0))
    return pl.pallas_call(
        functools.partial(_cmp_kernel, tq=tq, nc=nc),
        grid=(batch, nt),
        in_specs=[row(NSA_WIDTH), cblk, cblk, row(LANE)],
        out_specs=[row(NSA_WIDTH),
                   pl.BlockSpec((1, NSA_KV_HEADS, nc, tq), lambda b, i: (b, 0, 0, i))],
        out_shape=[jax.ShapeDtypeStruct((batch * seq, NSA_WIDTH), F32),
                   jax.ShapeDtypeStruct((batch, NSA_KV_HEADS, nc, seq), F32)],
        compiler_params=pltpu.CompilerParams(dimension_semantics=("arbitrary", "arbitrary"),
                                             vmem_limit_bytes=VMEM_LIMIT),
        name="nsa_cmp_prompt",
    )(nq, kcc.reshape(batch, nc, KV_W), vcc.reshape(batch, nc, KV_W), misc)


AUG_ROWS = NSA_HEAD_DIM + SUBLANE
BF16_ROWS = 2 * SUBLANE
SLC_TILES_PER_STEP = 4


def _slc_kernel(q_ref, ocmp_ref, gate_ref, selt_ref, e0t_ref, ks_ref, vs_ref, kw_ref, vw_ref, o_ref,
                q4t_scr, acc_scr, m_scr, comb_scr, *, tq, tk):
    q0 = pl.program_id(1) * tq
    n_kt = (q0 + tq - 1) // tk + 1
    t = q0 + lax.broadcasted_iota(jnp.int32, (1, tq), 1)
    bpt = tk // SEL_BLOCK
    wlen = WINDOW + tq
    cols = NSA_GROUP * tq
    w0 = pl.multiple_of(jnp.maximum(q0 - WINDOW, 0), tq)
    wpos = w0 + lax.broadcasted_iota(jnp.int32, (wlen, 1), 0)
    wbias = jnp.where((wpos <= t) & (t - wpos < WINDOW), 0.0, NEG)
    qt = q_ref[...].astype(F32).T
    gate_t = gate_ref[...].T
    zero = jnp.zeros((NSA_HEAD_DIM, cols), BF16)

    def head_cols(a):
        return jnp.concatenate([a] * NSA_GROUP, axis=1)

    def with_ones(vt):
        return jnp.concatenate([vt, jnp.ones((SUBLANE, vt.shape[1]), BF16)], axis=0)

    groups = range(NSA_KV_HEADS)
    lanes = [slice(g * NSA_HEAD_DIM, (g + 1) * NSA_HEAD_DIM) for g in groups]
    for g in groups:
        q4t = jnp.concatenate(
            [qt[(g * NSA_GROUP + r) * NSA_HEAD_DIM:(g * NSA_GROUP + r + 1) * NSA_HEAD_DIM]
             for r in range(NSA_GROUP)], axis=1).astype(BF16)
        for slot in range(SLC_TILES_PER_STEP):
            q4t_scr[slot * NSA_KV_HEADS + g] = jnp.concatenate(
                [q4t, zero] if g == 0 else [zero, q4t], axis=0)
    acc_scr[...] = jnp.zeros(acc_scr.shape, F32)

    own_lanes = [(lax.broadcasted_iota(jnp.int32, (1, KV_W), 1) // NSA_HEAD_DIM) == g for g in groups]
    pad_rows = jnp.zeros((BF16_ROWS - bpt, cols), BF16)

    m_scr[...] = jnp.full(m_scr.shape, NEG, F32)

    def scores(j, g, diagonal, slot):
        k0 = pl.multiple_of(j * tk, tk)
        sb = selt_ref[0, g, pl.ds(pl.multiple_of(j * bpt, bpt), bpt), :]
        og = 1 - g
        qs = slot * NSA_KV_HEADS + g
        q4t_scr[qs, og * NSA_HEAD_DIM:og * NSA_HEAD_DIM + BF16_ROWS, :] = jnp.concatenate(
            [head_cols(sb).astype(BF16), pad_rows], axis=0)
        s = _mm(jnp.where(own_lanes[g], ks_ref[0, pl.ds(k0, tk), :], e0t_ref[g]), q4t_scr[qs])
        if diagonal:
            kpos = k0 + lax.broadcasted_iota(jnp.int32, (tk, 1), 0)
            s = jnp.where(kpos <= head_cols(t), s, NEG)
        return s

    def absorb(j, g, s):
        k0 = pl.multiple_of(j * tk, tk)
        m_old = m_scr[g]
        m_new = jnp.maximum(m_old, jnp.max(s, axis=0, keepdims=True))
        m_scr[g] = m_new
        p = jnp.exp(s - m_new).astype(BF16)
        acc_scr[g] = (jnp.exp(m_old - m_new) * acc_scr[g]
                      + _mm(with_ones(vs_ref[0, lanes[g], pl.ds(k0, tk)]), p))

    def run(tiles):
        items = [(j, g, d, slot) for slot, (j, d) in enumerate(tiles) for g in groups]
        ahead = 4
        pending = [scores(*it) for it in items[:ahead]]
        for k, (j, g, _, _) in enumerate(items):
            s = pending.pop(0)
            absorb(j, g, s)
            if k + ahead < len(items):
                pending.append(scores(*items[k + ahead]))

    n_past = n_kt - 1

    def full_step(i, carry):
        run([(SLC_TILES_PER_STEP * i + k, False) for k in range(SLC_TILES_PER_STEP)])
        return carry

    lax.fori_loop(0, n_past // SLC_TILES_PER_STEP, full_step, 0)
    for rem in range(SLC_TILES_PER_STEP):
        def tail(rem=rem):
            run([(n_past - rem + k, False) for k in range(rem)] + [(n_past, True)])
        pl.when(n_past % SLC_TILES_PER_STEP == rem)(tail)

    kw_rows = kw_ref[0, pl.ds(w0, wlen), :]
    for g in groups:
        acc = acc_scr[g]
        o_slc = acc[0:NSA_HEAD_DIM] / acc[NSA_HEAD_DIM:NSA_HEAD_DIM + 1]
        s = _mm(jnp.where(own_lanes[g], kw_rows, jnp.zeros_like(kw_rows)), q4t_scr[g]) + head_cols(wbias)
        p = jnp.exp(s - jnp.max(s, axis=0, keepdims=True)).astype(BF16)
        acc = _mm(with_ones(vw_ref[0, lanes[g], pl.ds(w0, wlen)]), p)
        o_win = acc[0:NSA_HEAD_DIM] / acc[NSA_HEAD_DIM:NSA_HEAD_DIM + 1]

        for r in range(NSA_GROUP):
            h = g * NSA_GROUP + r
            c = GATE_LANE0 + 3 * h
            cs = slice(r * tq, (r + 1) * tq)
            comb_scr[h * NSA_HEAD_DIM:(h + 1) * NSA_HEAD_DIM, :] = (
                o_slc[:, cs] * gate_t[c + 1:c + 2] + o_win[:, cs] * gate_t[c + 2:c + 3])
    o_ref[...] = ocmp_ref[...] + comb_scr[...].T


def _slc_prompt(nq, ocmp, misc, selt, ks, vs, kw, vw, *, batch, seq, tq, tk):
    nt = seq // tq
    nc = seq // SEL_BLOCK
    bpt = tk // SEL_BLOCK
    key_blk = jnp.arange(tk, dtype=jnp.int32)[None, :, None] // SEL_BLOCK
    other0 = (1 - jnp.arange(NSA_KV_HEADS, dtype=jnp.int32))[:, None, None] * NSA_HEAD_DIM
    e0t = (jnp.arange(KV_W, dtype=jnp.int32)[None, None, :] - other0 == key_blk).astype(BF16)
    assert tk % tq == 0 and seq >= WINDOW + tq and bpt <= BF16_ROWS
    cols = NSA_GROUP * tq
    row = lambda n: pl.BlockSpec((tq, n), lambda b, i: (b * nt + i, 0))
    k_rows = pl.BlockSpec((1, seq, KV_W), lambda b, i: (b, 0, 0), pipeline_mode=pl.Buffered(1))
    v_cols = pl.BlockSpec((1, KV_W, seq), lambda b, i: (b, 0, 0), pipeline_mode=pl.Buffered(1))
    return pl.pallas_call(
        functools.partial(_slc_kernel, tq=tq, tk=tk),
        grid=(batch, nt),
        in_specs=[row(NSA_WIDTH), row(NSA_WIDTH), row(LANE),
                  pl.BlockSpec((1, NSA_KV_HEADS, nc, tq), lambda b, i: (b, 0, 0, i)),
                  _const_spec((NSA_KV_HEADS, tk, KV_W)), k_rows, v_cols, k_rows, v_cols],
        out_specs=row(NSA_WIDTH),
        out_shape=jax.ShapeDtypeStruct((batch * seq, NSA_WIDTH), F32),
        scratch_shapes=[pltpu.VMEM((SLC_TILES_PER_STEP * NSA_KV_HEADS, KV_W, cols), BF16),
                        pltpu.VMEM((NSA_KV_HEADS, AUG_ROWS, cols), F32),
                        pltpu.VMEM((NSA_KV_HEADS, 1, cols), F32),
                        pltpu.VMEM((NSA_WIDTH, tq), F32)],
        compiler_params=pltpu.CompilerParams(dimension_semantics=("arbitrary", "arbitrary"),
                                             vmem_limit_bytes=VMEM_LIMIT),
        name="nsa_slc_prompt",
    )(nq, ocmp, misc, selt, e0t, ks.reshape(batch, seq, KV_W), vs, kw.reshape(batch, seq, KV_W), vw)


PAGES_PER_MM = SUBLANE // (PAGE_SIZE // CMP_BLOCK)


def _pool_cmp_kernel(pk_ref, pv_ref, wk_ref, wv_ref, sel_ref, ok_ref, ov_ref, *, pp):
    def quad(i, carry):
        for p_ref, w_ref, o_ref in ((pk_ref, wk_ref, ok_ref), (pv_ref, wv_ref, ov_ref)):
            acc = jnp.zeros((SUBLANE, KV_W), F32)
            for k in range(PAGES_PER_MM):
                y = (p_ref[i * PAGES_PER_MM + k].reshape(KV_W, PAGE_SIZE) * w_ref[...]).astype(BF16)
                acc = acc + _nt(sel_ref[k], y)
            o_ref[pl.ds(pl.multiple_of(i * SUBLANE, SUBLANE), SUBLANE), :] = acc
        return carry

    lax.fori_loop(0, pp // PAGES_PER_MM, quad, 0, unroll=8)


def _pool_compress(pool_k, pool_v, wck_t, wcv_t, *, pp):
    n_pool = pool_k.shape[0]
    bpp = PAGE_SIZE // CMP_BLOCK
    rowid = jnp.arange(SUBLANE, dtype=jnp.int32)[None, :, None]
    want = (bpp * jnp.arange(PAGES_PER_MM, dtype=jnp.int32)[:, None, None]
            + jnp.arange(PAGE_SIZE, dtype=jnp.int32)[None, None, :] // CMP_BLOCK)
    sel = (rowid == want).astype(BF16)
    pg = pl.BlockSpec((pp, NSA_KV_HEADS, NSA_HEAD_DIM, PAGE_SIZE), lambda i: (i, 0, 0, 0))
    og = pl.BlockSpec((pp * bpp, KV_W), lambda i: (i, 0))
    return pl.pallas_call(
        functools.partial(_pool_cmp_kernel, pp=pp),
        grid=(n_pool // pp,),
        in_specs=[pg, pg, _const_spec((KV_W, PAGE_SIZE)), _const_spec((KV_W, PAGE_SIZE)),
                  _const_spec(sel.shape)],
        out_specs=[og, og],
        out_shape=[jax.ShapeDtypeStruct((n_pool * bpp, KV_W), F32)] * 2,
        compiler_params=pltpu.CompilerParams(dimension_semantics=("arbitrary",),
                                             vmem_limit_bytes=VMEM_LIMIT),
        name="pool_compress",
    )(pool_k, pool_v, wck_t, wcv_t, sel)


def _cmp_step_kernel(pt_ref, q_ref, kcn_ref, vcn_ref, gate_ref, wk_ref, wv_ref, pkc_ref, pvc_ref,
                     o_ref, idx_ref, kbuf, vbuf, *, n_pages, ncp, bb):
    b0 = pl.program_id(0) * bb
    bpp = PAGE_SIZE // CMP_BLOCK
    n_past = n_pages * bpp
    t = n_pages * PAGE_SIZE
    blk = lax.broadcasted_iota(jnp.int32, (1, ncp), 1)
    cmask = blk * CMP_BLOCK + (CMP_BLOCK - 1) <= t
    cur = t // SEL_BLOCK
    forced = (blk == 0) | (blk == cur) | (blk == cur - 1)
    scores = []
    for i in range(bb):
        def gather(j, carry):
            p = pt_ref[b0 + i, j]
            for src, dst in ((pkc_ref, kbuf), (pvc_ref, vbuf)):
                for u in range(bpp):
                    dst[i, pl.ds(j * bpp + u, 1), :] = src[pl.ds(p * bpp + u, 1), :]
            return carry

        lax.fori_loop(0, n_pages, gather, 0)
        kbuf[i, n_past:ncp, :] = jnp.zeros((ncp - n_past, KV_W), F32)
        vbuf[i, n_past:ncp, :] = jnp.zeros((ncp - n_past, KV_W), F32)
        kbuf[i, n_past:n_past + 1, :] = kcn_ref[i:i + 1, :] * wk_ref[0:1, :]
        vbuf[i, n_past:n_past + 1, :] = vcn_ref[i:i + 1, :] * wv_ref[0:1, :]
        kc = kbuf[i].astype(BF16)
        vc = vbuf[i].astype(BF16)
        qrow = q_ref[i:i + 1, :]
        for g in range(NSA_KV_HEADS):
            gl = slice(g * NSA_HEAD_DIM, (g + 1) * NSA_HEAD_DIM)
            p = _masked_softmax(_nt(_group_rows(qrow, g), kc[:, gl]), cmask)
            o4 = _mm(p.astype(BF16), vc[:, gl])
            for r in range(NSA_GROUP):
                h = g * NSA_GROUP + r
                c = GATE_LANE0 + 3 * h
                o_ref[i:i + 1, h * NSA_HEAD_DIM:(h + 1) * NSA_HEAD_DIM] = (
                    o4[r:r + 1] * gate_ref[i:i + 1, c:c + 1])
            score = jnp.where(forced, float(NSA_GROUP + 1),
                              jnp.sum(p[0:NSA_GROUP], axis=0, keepdims=True))
            score = jnp.where(blk > cur, -1.0, score)
            scores.append(jnp.where(blk > n_past, -3.0, score))
    score = jnp.concatenate(scores, axis=0)
    lane = lax.broadcasted_iota(jnp.int32, (1, LANE), 1)
    idx = jnp.zeros((bb * NSA_KV_HEADS, LANE), jnp.int32)
    for n in range(TOP_N):
        j = _first_max(score, blk)
        idx = jnp.where(lane == n, j, idx)
        score = jnp.where(blk == j, -4.0, score)
    idx_ref[...] = idx


def _cmp_step(page_table, nq, kcn, vcn, misc, wck, wcv, pkc, pvc, *, bb):
    n, n_pages = page_table.shape
    bpp = PAGE_SIZE // CMP_BLOCK
    ncp = -(-(n_pages * bpp + 1) // LANE) * LANE
    row = lambda w: pl.BlockSpec((bb, w), lambda i, pt: (i, 0))
    cst = lambda shape: pl.BlockSpec(shape, lambda i, pt: (0, 0), pipeline_mode=pl.Buffered(1))
    return pl.pallas_call(
        functools.partial(_cmp_step_kernel, n_pages=n_pages, ncp=ncp, bb=bb),
        grid_spec=pltpu.PrefetchScalarGridSpec(
            num_scalar_prefetch=1, grid=(n // bb,),
            in_specs=[row(NSA_WIDTH), row(KV_W), row(KV_W), row(LANE),
                      cst((CMP_BLOCK, KV_W)), cst((CMP_BLOCK, KV_W)),
                      cst(pkc.shape), cst(pvc.shape)],
            out_specs=[row(NSA_WIDTH), pl.BlockSpec((bb * NSA_KV_HEADS, LANE), lambda i, pt: (i, 0))],
            scratch_shapes=[pltpu.VMEM((bb, ncp, KV_W), F32), pltpu.VMEM((bb, ncp, KV_W), F32)]),
        out_shape=[jax.ShapeDtypeStruct((n, NSA_WIDTH), F32),
                   jax.ShapeDtypeStruct((n * NSA_KV_HEADS, LANE), jnp.int32)],
        compiler_params=pltpu.CompilerParams(dimension_semantics=("arbitrary",),
                                             vmem_limit_bytes=VMEM_LIMIT),
        name="nsa_cmp_step",
    )(page_table, nq.astype(F32), kcn, vcn, misc, wck, wcv, pkc, pvc)


def _slc_step_kernel(pt_ref, idx_ref, q_ref, ocmp_ref, gate_ref, ksn_ref, vsn_ref, kwn_ref, vwn_ref,
                     kwb_ref, vwb_ref, pks_ref, pvs_ref, o_ref, kbuf, vbuf, sem, *, n_pages, wb):
    b = pl.program_id(0)
    bpp = PAGE_SIZE // SEL_BLOCK
    n_past = n_pages * bpp
    t = n_pages * PAGE_SIZE
    n_sel = NSA_KV_HEADS * TOP_N

    def copies(i):
        page = pt_ref[b, jnp.minimum(idx_ref[b, i], n_past - 1) // bpp]
        g = i // TOP_N
        return (pltpu.make_async_copy(pks_ref.at[page, g], kbuf.at[i], sem.at[0, i]),
                pltpu.make_async_copy(pvs_ref.at[page, g], vbuf.at[i], sem.at[1, i]))

    for i in range(n_sel):
        for cp in copies(i):
            cp.start()
    for i in range(n_sel):
        for cp in copies(i):
            cp.wait()

    kidx = lax.broadcasted_iota(jnp.int32, (1, TOP_N * PAGE_SIZE), 1)
    krow = kidx % PAGE_SIZE
    wpos = t - wb + lax.broadcasted_iota(jnp.int32, (1, wb), 1)
    wmask = (wpos >= 0) & (wpos <= t) & (t - wpos < WINDOW)
    qrow = q_ref[0].astype(F32)

    def with_new_key(s_old, mask_old, vt_old, k_new, v_new, q4, use_new):
        sn = jnp.sum(q4.astype(F32) * k_new.astype(BF16).astype(F32), axis=-1, keepdims=True)
        sn = jnp.where(use_new, sn, NEG)
        so = jnp.where(mask_old, s_old, NEG)
        m = jnp.maximum(jnp.max(so, axis=-1, keepdims=True), sn)
        eo = jnp.where(mask_old, jnp.exp(so - m), 0.0)
        en = jnp.where(use_new, jnp.exp(sn - m), 0.0)
        den = jnp.maximum(jnp.sum(eo, axis=-1, keepdims=True) + en, 1e-30)
        return (_nt((eo / den).astype(BF16), vt_old)
                + (en / den).astype(BF16).astype(F32) * v_new.astype(BF16).astype(F32))

    for g in range(NSA_KV_HEADS):
        gl = slice(g * NSA_HEAD_DIM, (g + 1) * NSA_HEAD_DIM)
        q4 = _group_rows(qrow, g)
        kblk = jnp.zeros((1, TOP_N * PAGE_SIZE), jnp.int32)
        has_new = False
        for n in range(TOP_N):
            blk = idx_ref[b, g * TOP_N + n]
            has_new = jnp.logical_or(has_new, blk >= n_past)
            kblk = jnp.where(kidx // PAGE_SIZE == n, blk, kblk)
        valid = ((krow // SEL_BLOCK == kblk % bpp) & (kblk < n_past)
                 & ((kblk // bpp) * PAGE_SIZE + krow <= t))
        kt = jnp.concatenate([kbuf[g * TOP_N + n] for n in range(TOP_N)], axis=1).astype(BF16)
        vt = jnp.concatenate([vbuf[g * TOP_N + n] for n in range(TOP_N)], axis=1).astype(BF16)
        o_slc = with_new_key(_mm(q4, kt), valid, vt, ksn_ref[0, :, gl], vsn_ref[0, :, gl], q4, has_new)
        o_win = with_new_key(_mm(q4, kwb_ref[0, g].astype(BF16)), wmask, vwb_ref[0, g].astype(BF16),
                             kwn_ref[0, :, gl], vwn_ref[0, :, gl], q4, True)
        for r in range(NSA_GROUP):
            h = g * NSA_GROUP + r
            hl = slice(h * NSA_HEAD_DIM, (h + 1) * NSA_HEAD_DIM)
            c = GATE_LANE0 + 3 * h
            o_ref[0, :, hl] = (ocmp_ref[0, :, hl] + o_slc[r:r + 1] * gate_ref[0, :, c + 1:c + 2]
                               + o_win[r:r + 1] * gate_ref[0, :, c + 2:c + 3])


def _slc_step(page_table, idx, nq, ocmp, misc, ksn, vsn, kwn, vwn, buf_kw, buf_vw, pool_ks, pool_vs):
    n, n_pages = page_table.shape
    wb = buf_kw.shape[-1]
    n_sel = NSA_KV_HEADS * TOP_N
    r3 = lambda a: a.reshape(n, 1, a.shape[-1])
    one = lambda w: pl.BlockSpec((1, 1, w), lambda i, pt, ix: (i, 0, 0))
    win = pl.BlockSpec((1, NSA_KV_HEADS, NSA_HEAD_DIM, wb), lambda i, pt, ix: (i, 0, 0, 0))
    hbm = pl.BlockSpec(memory_space=pl.ANY)
    return pl.pallas_call(
        functools.partial(_slc_step_kernel, n_pages=n_pages, wb=wb),
        grid_spec=pltpu.PrefetchScalarGridSpec(
            num_scalar_prefetch=2, grid=(n,),
            in_specs=[one(NSA_WIDTH), one(NSA_WIDTH), one(LANE), one(KV_W), one(KV_W), one(KV_W),
                      one(KV_W), win, win, hbm, hbm],
            out_specs=one(NSA_WIDTH),
            scratch_shapes=[pltpu.VMEM((n_sel, NSA_HEAD_DIM, PAGE_SIZE), F32),
                            pltpu.VMEM((n_sel, NSA_HEAD_DIM, PAGE_SIZE), F32),
                            pltpu.SemaphoreType.DMA((2, n_sel))]),
        out_shape=jax.ShapeDtypeStruct((n, 1, NSA_WIDTH), F32),
        compiler_params=pltpu.CompilerParams(dimension_semantics=("arbitrary",),
                                             vmem_limit_bytes=VMEM_LIMIT),
        name="nsa_slc_step",
    )(page_table, idx, r3(nq), r3(ocmp), r3(misc), r3(ksn), r3(vsn), r3(kwn), r3(vwn),
      buf_kw, buf_vw, pool_ks, pool_vs)


def _post_kernel(x_ref, og_ref, on_ref, pe_ref, p0_ref, p1_ref, wo_ref, gf_ref, wup_ref, wcv_ref,
                 bcv_ref, wdn_ref, gp_ref, wple_ref, wpg_ref, gfin_ref, y_ref, ut_ref,
                 carry, ext, act_scr, *, tm, tail, shift):
    attn = (_mm(og_ref[...].astype(BF16), wo_ref[0:GLA_WIDTH, :])
            + _mm(on_ref[...].astype(BF16), wo_ref[GLA_WIDTH:, :]))
    h = x_ref[...] + attn
    hn = _rms(h, gf_ref[...]).astype(BF16)
    if shift:
        @pl.when(pl.program_id(1) == 0)
        def _():
            carry[...] = jnp.zeros_like(carry)

    for j in range(N_FF_CHUNKS):
        u = _mm(hn, wup_ref[j])
        wc = wcv_ref[j]
        if shift:
            e = j % 2
            ext[e, 0:SUBLANE, :] = carry[j]
            ext[e, SUBLANE:, :] = u
            um2 = ext[e, pl.ds(SUBLANE - 2, tm), :]
            um1 = ext[e, pl.ds(SUBLANE - 1, tm), :]
            carry[j] = u[tm - SUBLANE:, :]
        else:
            um2 = p0_ref[j]
            um1 = p1_ref[j]
        c = bcv_ref[j] + um2 * wc[0:1] + um1 * wc[1:2] + u * wc[2:3]
        ut_ref[0, j] = u[tm - tail:, :]
        a = c[:, :FF_CHUNK]
        act_scr[:, j * FF_CHUNK:(j + 1) * FF_CHUNK] = (a * _sigmoid(a) * c[:, FF_CHUNK:]).astype(BF16)
    h = h + _mm(act_scr[...], wdn_ref[...])
    gate = _sigmoid(_mm(_rms(h, gp_ref[...]).astype(BF16), wpg_ref[...]))
    h = h + _mm(pe_ref[...].astype(BF16), wple_ref[...]) * gate
    y_ref[...] = _rms(h, gfin_ref[...])


def _post(x2d, ogla, onsa, pe2d, prev, pw, g_final, *, batch, tm, shift):
    m = x2d.shape[0]
    nt = m // (batch * tm)
    tail = SUBLANE if shift else tm
    row = lambda n: pl.BlockSpec((tm, n), lambda b, i: (b * nt + i, 0))
    cst = lambda shape: pl.BlockSpec(shape, lambda b, i: (0,) * len(shape),
                                     pipeline_mode=pl.Buffered(1))
    if shift:
        prev_spec = cst((1, 1, LANE))
        p0 = p1 = jnp.zeros((1, 1, LANE), F32)
    else:
        prev_spec = pl.BlockSpec((N_FF_CHUNKS, tm, 2 * FF_CHUNK), lambda b, i: (0, b * nt + i, 0))
        p0, p1 = prev
    return pl.pallas_call(
        functools.partial(_post_kernel, tm=tm, tail=tail, shift=shift),
        grid=(batch, nt),
        in_specs=[row(D_MODEL), row(GLA_WIDTH), row(NSA_WIDTH), row(PLE_DIM), prev_spec, prev_spec,
                  cst((D_MODEL, D_MODEL)), cst((1, D_MODEL)),
                  cst((N_FF_CHUNKS, D_MODEL, 2 * FF_CHUNK)), cst((N_FF_CHUNKS, CONV_W, 2 * FF_CHUNK)),
                  cst((N_FF_CHUNKS, 1, 2 * FF_CHUNK)), cst((D_FF, D_MODEL)),
                  cst((1, D_MODEL)), cst((PLE_DIM, D_MODEL)), cst((D_MODEL, D_MODEL)),
                  cst((1, D_MODEL))],
        out_specs=[row(D_MODEL),
                   pl.BlockSpec((1, N_FF_CHUNKS, tail, 2 * FF_CHUNK), lambda b, i: (b, 0, 0, 0))],
        out_shape=[jax.ShapeDtypeStruct((m, D_MODEL), F32),
                   jax.ShapeDtypeStruct((batch, N_FF_CHUNKS, tail, 2 * FF_CHUNK), F32)],
        scratch_shapes=[pltpu.VMEM((N_FF_CHUNKS, SUBLANE, 2 * FF_CHUNK), F32),
                        pltpu.VMEM((2, tm + SUBLANE, 2 * FF_CHUNK), F32),
                        pltpu.VMEM((tm, D_FF), BF16)],
        compiler_params=pltpu.CompilerParams(dimension_semantics=("arbitrary", "arbitrary"),
                                             vmem_limit_bytes=VMEM_LIMIT),
        name="post",
    )(x2d, ogla, onsa, pe2d, p0, p1, pw["w_o"], pw["g_ffn"], pw["w_up"], pw["w_conv"], pw["b_conv"],
      pw["w_down"], pw["g_ple"], pw["w_ple"], pw["w_ple_gate"], g_final)


def _ff_chunked(a):
    lead = a.shape[:-1]
    a = a.reshape(*lead, 2, N_FF_CHUNKS, FF_CHUNK)
    a = jnp.moveaxis(a, -2, 0)
    return a.reshape(N_FF_CHUNKS, *lead, 2 * FF_CHUNK)


def _ff_unchunked(a):
    r = a.shape[1]
    return a.reshape(N_FF_CHUNKS, r, 2, FF_CHUNK).transpose(1, 2, 0, 3).reshape(r, 2 * D_FF)


def _pack_weights(i, g_attn, w_in, w_gla_gate, b_gla_gate, g_gla_out, b_nsa_gate, w_cmp_k, w_cmp_v,
                  w_o, g_ffn, w_up, w_conv, b_conv, w_down, g_ple, w_ple, w_ple_gate):
    w = w_in[i]
    o_glr = 2 * GLA_QK + 2 * GLA_WIDTH
    o_nq = o_glr + GLA_GATE_RANK
    o_ng = o_nq + NSA_WIDTH + 6 * KV_W
    pad = jnp.zeros((D_MODEL, LANE - GLA_GATE_RANK - 3 * NSA_HEADS), w.dtype)
    w_pack = jnp.concatenate([w[:, :o_glr], w[:, o_nq:o_ng], w[:, o_glr:o_nq], w[:, o_ng:], pad], axis=1)
    w_gate = jnp.zeros((LANE, GLA_QK), F32).at[:GLA_GATE_RANK].set(w_gla_gate[i])
    b_misc = jnp.zeros((1, LANE), F32).at[0, GATE_LANE0:GATE_LANE0 + 3 * NSA_HEADS].set(b_nsa_gate[i])
    return {
        "g_attn": g_attn[i][None], "w_in": w_pack.astype(BF16), "w_gate": w_gate.astype(BF16),
        "b_gate": b_gla_gate[i][None], "b_misc": b_misc, "g_gla_out": g_gla_out[i][None],
        "w_ck": jnp.repeat(w_cmp_k[i], NSA_HEAD_DIM, axis=1),
        "w_cv": jnp.repeat(w_cmp_v[i], NSA_HEAD_DIM, axis=1),
        "w_ck_t": jnp.tile(jnp.repeat(w_cmp_k[i].T, NSA_HEAD_DIM, axis=0), (1, PAGE_SIZE // CMP_BLOCK)),
        "w_cv_t": jnp.tile(jnp.repeat(w_cmp_v[i].T, NSA_HEAD_DIM, axis=0), (1, PAGE_SIZE // CMP_BLOCK)),
        "w_o": w_o[i].astype(BF16), "g_ffn": g_ffn[i][None],
        "w_up": _ff_chunked(w_up[i]).astype(BF16), "w_conv": _ff_chunked(w_conv[i]),
        "b_conv": _ff_chunked(b_conv[i][None]),
        "w_down": w_down[i].astype(BF16),
        "g_ple": g_ple[i][None], "w_ple": w_ple[i].astype(BF16),
        "w_ple_gate": w_ple_gate[i].astype(BF16),
    }


def _rope_tables(pos):
    half = ROT_DIM // 2
    inv = ROPE_THETA ** (-jnp.arange(half, dtype=F32) / half)
    ang = pos.astype(F32)[:, None] * inv[None, :]
    d = jnp.arange(LANE) % NSA_HEAD_DIM
    cos = jnp.cos(ang)[:, d % half]
    sin = jnp.sin(ang)[:, d % half]
    return (jnp.where(d < ROT_DIM, cos, 1.0), jnp.where(d < half, -sin, 0.0),
            jnp.where((d >= half) & (d < ROT_DIM), sin, 0.0))


def _kv4(a, b, t):
    return a.reshape(b, t, NSA_KV_HEADS, NSA_HEAD_DIM)


def _prompt_layer(x, pe, pw, g_final):
    b, t, _ = x.shape
    m = b * t
    tabs = _rope_tables(jnp.arange(t, dtype=jnp.int32))
    (gqk, gv, gr, la, nq, misc, kc, vc, ks, vs, kw, vw, ks16, vs16, kw16, vw16, kcc, vcc) = _inproj(
        x.reshape(m, D_MODEL), tabs, pw, tm=512, batch=b)
    ogla, h_gla = _gla_prompt(gqk, gv, gr, la, pw["g_gla_out"], batch=b, seq=t, tg=512)
    ocmp, selt = _cmp_prompt(nq, kcc, vcc, misc, batch=b, seq=t, tq=256)
    onsa = _slc_prompt(nq, ocmp, misc, selt, ks16, vs16, kw16, vw16, batch=b, seq=t, tq=128, tk=512)
    y, utail = _post(x.reshape(m, D_MODEL), ogla, onsa, pe.reshape(m, PLE_DIM), None, pw, g_final,
                     batch=b, tm=512, shift=True)
    conv_new = jax.vmap(_ff_unchunked)(utail)[:, SUBLANE - (CONV_W - 1):]
    nw = min(WINDOW, t)

    def rows(a):
        return a.reshape(b, NSA_KV_HEADS, NSA_HEAD_DIM, t).transpose(0, 3, 1, 2)

    state = (rows(kc), rows(vc), rows(ks), rows(vs), rows(kw)[:, t - nw:], rows(vw)[:, t - nw:],
             h_gla, conv_new)
    return y.reshape(b, t, D_MODEL), state


def _sample_layer(x, pe, pw, g_final, pools, bufs, h0, conv_prev, page_table):
    n, t, _ = x.shape
    assert t == 1, "the sample group is written for one new token per request"
    n_pages = page_table.shape[1]
    pos = jnp.full((n,), n_pages * PAGE_SIZE, jnp.int32)
    tabs = _rope_tables(pos)
    x2d = x.reshape(n, D_MODEL)
    (gqk, gv, gr, la, nq, misc, kc, vc, ks, vs, kw, vw) = _inproj(x2d, tabs, pw, tm=n)
    ogla, h_gla = _gla_step(gqk, gv, gr, la, pw["g_gla_out"], h0, bb=SUBLANE)
    pool_kc, pool_vc, pool_ks, pool_vs = [p.transpose(0, 2, 3, 1) for p in pools]
    n_pool = pool_kc.shape[0]
    pp = max(p for p in range(PAGES_PER_MM, 65, PAGES_PER_MM) if n_pool % p == 0)
    pkc, pvc = _pool_compress(pool_kc, pool_vc, pw["w_ck_t"], pw["w_cv_t"], pp=pp)
    ocmp, idx = _cmp_step(page_table, nq, kc, vc, misc, pw["w_ck"], pw["w_cv"], pkc, pvc, bb=SUBLANE)
    idx = idx.reshape(n, NSA_KV_HEADS, LANE)[:, :, :TOP_N].reshape(n, NSA_KV_HEADS * TOP_N)
    buf_kw, buf_vw = bufs
    onsa = _slc_step(page_table, idx, nq, ocmp, misc, ks, vs, kw, vw,
                     buf_kw.transpose(0, 2, 3, 1), buf_vw.transpose(0, 2, 3, 1), pool_ks, pool_vs)
    prev = (_ff_chunked(conv_prev[:, 0]), _ff_chunked(conv_prev[:, 1]))
    y, utail = _post(x2d, ogla, onsa.reshape(n, NSA_WIDTH), pe.reshape(n, PLE_DIM), prev, pw, g_final,
                     batch=1, tm=n, shift=False)
    u = _ff_unchunked(utail[0])
    conv_new = jnp.stack([conv_prev[:, 1], u], axis=1)
    wb = buf_kw.shape[1]
    nw = min(WINDOW, wb + 1)
    kwin = jnp.concatenate([buf_kw, _kv4(kw, n, 1)], axis=1)[:, wb + 1 - nw:]
    vwin = jnp.concatenate([buf_vw, _kv4(vw, n, 1)], axis=1)[:, wb + 1 - nw:]
    state = (_kv4(kc, n, 1), _kv4(vc, n, 1), _kv4(ks, n, 1), _kv4(vs, n, 1), kwin, vwin, h_gla,
             conv_new)
    return y.reshape(n, 1, D_MODEL), state


def kernel(x_prompt, x_sample, p_prompt, p_sample, cache_k_cmp, cache_v_cmp, cache_k_slc, cache_v_slc, cache_k_win, cache_v_win, state_gla, state_conv, page_table, g_attn, w_in, w_gla_gate, b_gla_gate, g_gla_out, b_nsa_gate, w_cmp_k, w_cmp_v, w_o, g_ffn, w_up, w_conv, b_conv, w_down, g_ple, w_ple, w_ple_gate, g_final):
    depth = w_in.shape[0]
    assert depth == 1, "the final norm is fused into the layer kernel; written for a one-layer trunk"
    pw = _pack_weights(0, g_attn, w_in, w_gla_gate, b_gla_gate, g_gla_out, b_nsa_gate, w_cmp_k,
                       w_cmp_v, w_o, g_ffn, w_up, w_conv, b_conv, w_down, g_ple, w_ple, w_ple_gate)
    gfin = g_final[None]
    y_p, st_p = _prompt_layer(x_prompt, p_prompt[0], pw, gfin)
    y_s, st_s = _sample_layer(
        x_sample, p_sample[0], pw, gfin,
        (cache_k_cmp[0], cache_v_cmp[0], cache_k_slc[0], cache_v_slc[0]),
        (cache_k_win[0], cache_v_win[0]), state_gla[0], state_conv[0], page_table)
    return (y_p, y_s, *[a[None] for a in st_p], *[a[None] for a in st_s])
```

```python
import functools

import jax
import jax.numpy as jnp
from jax import lax
from jax.experimental import pallas as pl
from jax.experimental.pallas import tpu as pltpu

F32 = jnp.float32
BF16 = jnp.bfloat16

D_MODEL = 1024
PAGE_SIZE = 128
GLA_WIDTH = 512
GLA_HEADS = 4
GLA_QK = 256
GLA_DK = 64
GLA_DV = 128
GLA_GATE_RANK = 16
GLA_TAU = 16.0
GLA_CHUNK = 64
NSA_WIDTH = 512
NSA_HEADS = 8
NSA_HEAD_DIM = 64
NSA_KV_HEADS = 2
NSA_GROUP = 4
KV_W = 128
CMP_BLOCK = 64
SEL_BLOCK = 64
TOP_N = 16
N_FORCED = 3
WINDOW = 512
ROT_DIM = 16
ROPE_THETA = 500000.0
D_FF = 2816
CONV_W = 3
PLE_DIM = 256
EPS = 1e-6

LANE = 128
SUBLANE = 8
NEG = -1e30
LOG2E = 1.4426950408889634
FF_CHUNK = 256
N_FF_CHUNKS = D_FF // FF_CHUNK
VMEM_LIMIT = 56 * 1024 * 1024

C_GQ, C_GK, C_GV, C_GR, C_NQ = 0, 256, 512, 1024, 1536
C_KC, C_VC, C_KS, C_VS, C_KW, C_VW, C_MISC = 2048, 2176, 2304, 2432, 2560, 2688, 2816
D_PACK = 2944
GATE_LANE0 = GLA_GATE_RANK


def _nt(a, b):
    return lax.dot_general(a, b, (((1,), (1,)), ((), ())), preferred_element_type=F32)


def _tn(a, b):
    return lax.dot_general(a, b, (((0,), (0,)), ((), ())), preferred_element_type=F32)


def _mm(a, b):
    return jnp.dot(a, b, preferred_element_type=F32)


def _rms(x, g):
    return x * lax.rsqrt(jnp.mean(x * x, axis=-1, keepdims=True) + EPS) * g


def _sigmoid(x):
    return 1.0 / (1.0 + jnp.exp(-x))


def _group_rows(qrow, g):
    parts = [qrow[:, (g * NSA_GROUP + r) * NSA_HEAD_DIM:(g * NSA_GROUP + r + 1) * NSA_HEAD_DIM]
             for r in range(NSA_GROUP)]
    parts.append(jnp.zeros((SUBLANE - NSA_GROUP, NSA_HEAD_DIM), F32))
    return jnp.concatenate(parts, axis=0).astype(BF16)


def _const_spec(shape):
    nd = len(shape)
    return pl.BlockSpec(shape, lambda *_: (0,) * nd, pipeline_mode=pl.Buffered(1))


def _inproj_kernel(x_ref, g_ref, w_ref, wgate_ref, bgate_ref, bmisc_ref, cos_ref, sa_ref, sb_ref,
                   wck_ref, wcv_ref, *out_refs, prompt):
    gqk_ref, gv_ref, gr_ref, la_ref, nq_ref, misc_ref = out_refs[:6]
    kv_refs = out_refs[6:12]
    xn = _rms(x_ref[...], g_ref[...]).astype(BF16)

    def proj(lo, hi):
        return _mm(xn, w_ref[:, lo:hi])

    cos, sa, sb = cos_ref[...], sa_ref[...], sb_ref[...]

    def rope(z):
        return z * cos + pltpu.roll(z, LANE - ROT_DIM // 2, 1) * sa + pltpu.roll(z, ROT_DIM // 2, 1) * sb

    gqk_ref[...] = proj(C_GQ, C_GV)
    gv_ref[...] = proj(C_GV, C_GR)
    gr_ref[...] = proj(C_GR, C_NQ)
    for j in range(NSA_WIDTH // LANE):
        z = proj(C_NQ + j * LANE, C_NQ + (j + 1) * LANE)
        nq_ref[:, j * LANE:(j + 1) * LANE] = (rope(z) * (NSA_HEAD_DIM ** -0.5)).astype(BF16)
    kv = [rope(proj(C_KC, C_VC)), proj(C_VC, C_KS), rope(proj(C_KS, C_VS)), proj(C_VS, C_KW),
          rope(proj(C_KW, C_VW)), proj(C_VW, C_MISC)]
    z = proj(C_MISC, D_PACK)
    misc_ref[...] = _sigmoid(z + bmisc_ref[...])
    pre = _mm(z.astype(BF16), wgate_ref[...]) + bgate_ref[...]
    la_ref[...] = (jnp.minimum(pre, 0.0) - jnp.log(1.0 + jnp.exp(-jnp.abs(pre)))) * (1.0 / GLA_TAU)
    if not prompt:
        for ref, a in zip(kv_refs, kv):
            ref[...] = a
        return
    kv_t = [a.T for a in kv]
    for ref, a in zip(kv_refs, kv_t):
        ref[0] = a
    ks16_ref, vs16_ref, kw16_ref, vw16_ref, kcc_ref, vcc_ref = out_refs[12:]
    ks16_ref[...] = kv[2].astype(BF16)
    kw16_ref[...] = kv[4].astype(BF16)
    vs16_ref[0] = kv_t[3].astype(BF16)
    vw16_ref[0] = kv_t[5].astype(BF16)
    nb = kv[0].shape[0] // CMP_BLOCK
    kcc_ref[...] = jnp.sum(kv[0].reshape(nb, CMP_BLOCK, KV_W) * wck_ref[...][None], axis=1)
    vcc_ref[...] = jnp.sum(kv[1].reshape(nb, CMP_BLOCK, KV_W) * wcv_ref[...][None], axis=1)


def _inproj(x2d, tabs, pw, *, tm, batch=None):
    m = x2d.shape[0]
    prompt = batch is not None
    n_tab = tabs[0].shape[0] // tm
    row = lambda n: pl.BlockSpec((tm, n), lambda i: (i, 0))
    tab = pl.BlockSpec((tm, LANE), lambda i: (i % n_tab, 0))
    widths = [2 * GLA_QK, GLA_WIDTH, GLA_WIDTH, GLA_QK, NSA_WIDTH, LANE]
    dtypes = [F32, F32, F32, F32, BF16, F32]
    out_shape = [jax.ShapeDtypeStruct((m, n), d) for n, d in zip(widths, dtypes)]
    out_specs = [row(n) for n in widths]
    if prompt:
        seq = m // batch
        nt = seq // tm
        kvt = pl.BlockSpec((1, KV_W, tm), lambda i: (i // nt, 0, i % nt))
        nb = tm // CMP_BLOCK
        rows16 = jax.ShapeDtypeStruct((m, KV_W), BF16)
        cols16 = jax.ShapeDtypeStruct((batch, KV_W, seq), BF16)
        out_shape += ([jax.ShapeDtypeStruct((batch, KV_W, seq), F32)] * 6
                      + [rows16, cols16, rows16, cols16]
                      + [jax.ShapeDtypeStruct((m // CMP_BLOCK, KV_W), F32)] * 2)
        out_specs += ([kvt] * 6 + [row(KV_W), kvt, row(KV_W), kvt]
                      + [pl.BlockSpec((nb, KV_W), lambda i: (i, 0))] * 2)
    else:
        out_shape += [jax.ShapeDtypeStruct((m, KV_W), F32)] * 6
        out_specs += [row(KV_W)] * 6
    return pl.pallas_call(
        functools.partial(_inproj_kernel, prompt=prompt),
        grid=(m // tm,),
        in_specs=[row(D_MODEL), _const_spec((1, D_MODEL)), _const_spec((D_MODEL, D_PACK)),
                  _const_spec((LANE, GLA_QK)), _const_spec((1, GLA_QK)), _const_spec((1, LANE)),
                  tab, tab, tab, _const_spec((CMP_BLOCK, KV_W)), _const_spec((CMP_BLOCK, KV_W))],
        out_specs=out_specs, out_shape=out_shape,
        compiler_params=pltpu.CompilerParams(dimension_semantics=("arbitrary",),
                                             vmem_limit_bytes=VMEM_LIMIT),
        name="inproj",
    )(x2d, pw["g_attn"], pw["w_in"], pw["w_gate"], pw["b_gate"], pw["b_misc"], *tabs,
      pw["w_ck"], pw["w_cv"])


def _gla_kernel(qk_ref, v_ref, r_ref, la_ref, g_ref, o_ref, hout_ref, ht_scr, o_scr, *, n_chunks):
    t = pl.program_id(1)

    @pl.when(t == 0)
    def _():
        ht_scr[...] = jnp.zeros_like(ht_scr)

    ri = lax.broadcasted_iota(jnp.int32, (GLA_CHUNK, GLA_CHUNK), 0)
    ci = lax.broadcasted_iota(jnp.int32, (GLA_CHUNK, GLA_CHUNK), 1)
    causal = ri >= ci
    tril = causal.astype(F32)

    def chunk(c, carry):
        r0 = pl.multiple_of(c * GLA_CHUNK, GLA_CHUNK)
        rows = pl.ds(r0, GLA_CHUNK)
        bc = jnp.dot(tril, la_ref[rows, :], preferred_element_type=F32,
                     precision=lax.Precision.HIGHEST)
        blast = bc[GLA_CHUNK - 1:GLA_CHUNK, :]
        q = qk_ref[rows, 0:GLA_QK]
        k = qk_ref[rows, GLA_QK:2 * GLA_QK]
        qe = (q * jnp.exp(bc) * (GLA_DK ** -0.5)).astype(BF16)
        ke = (k * jnp.exp(-bc)).astype(BF16)
        kd = (k * jnp.exp(blast - bc)).astype(BF16)
        decay = jnp.exp(blast)
        for h in range(GLA_HEADS):
            ks = slice(h * GLA_DK, (h + 1) * GLA_DK)
            vh = v_ref[rows, h * GLA_DV:(h + 1) * GLA_DV].astype(BF16)
            ht = ht_scr[h]
            att = jnp.where(causal, _nt(qe[:, ks], ke[:, ks]), 0.0).astype(BF16)
            o_scr[rows, h * GLA_DV:(h + 1) * GLA_DV] = _mm(att, vh) + _nt(qe[:, ks], ht.astype(BF16))
            ht_scr[h] = ht * decay[:, ks] + _tn(vh, kd[:, ks])
        return carry

    lax.fori_loop(0, n_chunks, chunk, 0, unroll=4)
    for h in range(GLA_HEADS):
        hs = slice(h * GLA_DV, (h + 1) * GLA_DV)
        r = r_ref[:, hs]
        o_ref[:, hs] = _rms(o_scr[:, hs], g_ref[:, hs]) * (r * _sigmoid(r))

    @pl.when(t == pl.num_programs(1) - 1)
    def _():
        for h in range(GLA_HEADS):
            hout_ref[0, h] = ht_scr[h].T


def _gla_prompt(gqk, gv, gr, la, g_out, *, batch, seq, tg):
    nt = seq // tg
    row = lambda n: pl.BlockSpec((tg, n), lambda b, t: (b * nt + t, 0))
    return pl.pallas_call(
        functools.partial(_gla_kernel, n_chunks=tg // GLA_CHUNK),
        grid=(batch, nt),
        in_specs=[row(2 * GLA_QK), row(GLA_WIDTH), row(GLA_WIDTH), row(GLA_QK),
                  pl.BlockSpec((1, GLA_WIDTH), lambda b, t: (0, 0))],
        out_specs=[row(GLA_WIDTH),
                   pl.BlockSpec((1, GLA_HEADS, GLA_DK, GLA_DV), lambda b, t: (b, 0, 0, 0))],
        out_shape=[jax.ShapeDtypeStruct((batch * seq, GLA_WIDTH), F32),
                   jax.ShapeDtypeStruct((batch, GLA_HEADS, GLA_DK, GLA_DV), F32)],
        scratch_shapes=[pltpu.VMEM((GLA_HEADS, GLA_DV, GLA_DK), F32),
                        pltpu.VMEM((tg, GLA_WIDTH), F32)],
        compiler_params=pltpu.CompilerParams(dimension_semantics=("arbitrary", "arbitrary"),
                                             vmem_limit_bytes=VMEM_LIMIT),
        name="gla_prompt",
    )(gqk, gv, gr, la, g_out)


def _gla_step_kernel(qt_ref, kt_ref, at_ref, v_ref, r_ref, g_ref, h0_ref, o_ref, h_ref, *, bb):
    for b in range(bb):
        for h in range(GLA_HEADS):
            ds_ = slice(h * GLA_DK, (h + 1) * GLA_DK)
            vs_ = slice(h * GLA_DV, (h + 1) * GLA_DV)
            qcol = qt_ref[0, ds_, b:b + 1] * (GLA_DK ** -0.5)
            kcol = kt_ref[0, ds_, b:b + 1]
            acol = jnp.exp(at_ref[0, ds_, b:b + 1])
            hn = acol * h0_ref[b, h] + kcol * v_ref[b:b + 1, vs_]
            h_ref[b, h] = hn
            o = jnp.sum(qcol * hn, axis=0, keepdims=True)
            r = r_ref[b:b + 1, vs_]
            o_ref[b:b + 1, vs_] = _rms(o, g_ref[:, vs_]) * (r * _sigmoid(r))


def _gla_step(gqk, gv, gr, la, g_out, h0, *, bb):
    n = gqk.shape[0]
    ns = n // bb

    def cols(a):
        return a.reshape(ns, bb, GLA_QK).transpose(0, 2, 1)

    colspec = pl.BlockSpec((1, GLA_QK, bb), lambda i: (i, 0, 0))
    row = pl.BlockSpec((bb, GLA_WIDTH), lambda i: (i, 0))
    st = pl.BlockSpec((bb, GLA_HEADS, GLA_DK, GLA_DV), lambda i: (i, 0, 0, 0))
    return pl.pallas_call(
        functools.partial(_gla_step_kernel, bb=bb),
        grid=(ns,),
        in_specs=[colspec, colspec, colspec, row, row,
                  pl.BlockSpec((1, GLA_WIDTH), lambda i: (0, 0)), st],
        out_specs=[row, st],
        out_shape=[jax.ShapeDtypeStruct((n, GLA_WIDTH), F32),
                   jax.ShapeDtypeStruct(h0.shape, F32)],
        compiler_params=pltpu.CompilerParams(dimension_semantics=("arbitrary",)),
        name="gla_step",
    )(cols(gqk[:, :GLA_QK]), cols(gqk[:, GLA_QK:]), cols(la), gv, gr, g_out, h0)


def _masked_softmax(s, mask, axis=-1):
    s = jnp.where(mask, s, NEG)
    m = jnp.max(s, axis=axis, keepdims=True)
    e = jnp.where(mask, jnp.exp(s - m), 0.0)
    return e / jnp.maximum(jnp.sum(e, axis=axis, keepdims=True), 1e-30)


def _first_max(score, blk, axis=-1):
    m = jnp.max(score, axis=axis, keepdims=True)
    return jnp.min(jnp.where(score == m, blk, score.shape[axis]), axis=axis, keepdims=True)


def _top_blocks(score, blk, n_pick, axis=-1):
    sel = jnp.zeros(score.shape, jnp.bool_)
    for _ in range(n_pick):
        pick = blk == _first_max(score, blk, axis)
        sel = jnp.logical_or(sel, pick)
        score = jnp.where(pick, -2.0, score)
    return sel


def _cmp_kernel(q_ref, kc_ref, vc_ref, gate_ref, o_ref, selt_ref, comb_scr, *, tq, nc):
    q0 = pl.program_id(1) * tq
    t = q0 + lax.broadcasted_iota(jnp.int32, (1, tq), 1)
    qt = q_ref[...].astype(F32).T
    gate_t = gate_ref[...].T

    def run(ncx):
        blk = lax.broadcasted_iota(jnp.int32, (ncx, 1), 0)
        cmask = blk * CMP_BLOCK + (CMP_BLOCK - 1) <= t
        cur = t // SEL_BLOCK
        forced = (blk == 0) | (blk == cur) | (blk == cur - 1)
        future = blk > cur
        kc = kc_ref[0, 0:ncx, :].astype(BF16)
        vct = vc_ref[0, 0:ncx, :].T.astype(BF16)
        scores = []
        for g in range(NSA_KV_HEADS):
            gl = slice(g * NSA_HEAD_DIM, (g + 1) * NSA_HEAD_DIM)
            psum = jnp.zeros((ncx, tq), F32)
            for r in range(NSA_GROUP):
                h = g * NSA_GROUP + r
                hl = slice(h * NSA_HEAD_DIM, (h + 1) * NSA_HEAD_DIM)
                p = _masked_softmax(_mm(kc[:, gl], qt[hl].astype(BF16)), cmask, axis=0)
                psum = psum + p
                c = GATE_LANE0 + 3 * h
                comb_scr[hl, :] = _mm(vct[gl], p.astype(BF16)) * gate_t[c:c + 1]
            scores.append(jnp.where(future, -1.0, jnp.where(forced, -2.0, psum)))
        sel = _top_blocks(jnp.concatenate(scores, axis=1), blk, TOP_N - N_FORCED, axis=0)
        for g in range(NSA_KV_HEADS):
            keep = jnp.logical_and(jnp.logical_or(sel[:, g * tq:(g + 1) * tq], forced),
                                   jnp.logical_not(future))
            selt_ref[0, g, 0:ncx, :] = jnp.where(keep, 0.0, NEG)
            if ncx < nc:
                selt_ref[0, g, ncx:nc, :] = jnp.full((nc - ncx, tq), NEG, F32)
        o_ref[...] = comb_scr[...].T

    half = nc // 2
    if half >= TOP_N:
        early = q0 + tq <= half * CMP_BLOCK
        pl.when(early)(lambda: run(half))
        pl.when(jnp.logical_not(early))(lambda: run(nc))
    else:
        run(nc)


def _cmp_prompt(nq, kcc, vcc, misc, *, batch, seq, tq):
    nc = seq // CMP_BLOCK
    assert nc >= TOP_N
    nt = seq // tq
    row = lambda n: pl.BlockSpec((tq, n), lambda b, i: (b * nt + i, 0))
    cblk = pl.BlockSpec((1, nc, KV_W), lambda b, i: (b, 0, 0))
    return pl.pallas_call(
        functools.partial(_cmp_kernel, tq=tq, nc=nc),
        grid=(batch, nt),
        in_specs=[row(NSA_WIDTH), cblk, cblk, row(LANE)],
        out_specs=[row(NSA_WIDTH),
                   pl.BlockSpec((1, NSA_KV_HEADS, nc, tq), lambda b, i: (b, 0, 0, i))],
        out_shape=[jax.ShapeDtypeStruct((batch * seq, NSA_WIDTH), F32),
                   jax.ShapeDtypeStruct((batch, NSA_KV_HEADS, nc, seq), F32)],
        scratch_shapes=[pltpu.VMEM((NSA_WIDTH, tq), F32)],
        compiler_params=pltpu.CompilerParams(dimension_semantics=("arbitrary", "arbitrary"),
                                             vmem_limit_bytes=VMEM_LIMIT),
        name="nsa_cmp_prompt",
    )(nq, kcc.reshape(batch, nc, KV_W), vcc.reshape(batch, nc, KV_W), misc)


AUG_ROWS = NSA_HEAD_DIM + SUBLANE
BF16_ROWS = 2 * SUBLANE
SLC_TILES_PER_STEP = 4


def _slc_kernel(q_ref, ocmp_ref, gate_ref, selt_ref, e0t_ref, ks_ref, vs_ref, kw_ref, vw_ref, o_ref,
                q4t_scr, acc_scr, m_scr, comb_scr, *, tq, tk):
    q0 = pl.program_id(1) * tq
    n_kt = (q0 + tq - 1) // tk + 1
    t = q0 + lax.broadcasted_iota(jnp.int32, (1, tq), 1)
    bpt = tk // SEL_BLOCK
    wlen = WINDOW + tq
    cols = NSA_GROUP * tq
    w0 = pl.multiple_of(jnp.maximum(q0 - WINDOW, 0), tq)
    wpos = w0 + lax.broadcasted_iota(jnp.int32, (wlen, 1), 0)
    wbias = jnp.where((wpos <= t) & (t - wpos < WINDOW), 0.0, NEG)
    qt = q_ref[...].astype(F32).T * LOG2E
    gate_t = gate_ref[...].T
    zero = jnp.zeros((NSA_HEAD_DIM, cols), BF16)

    def head_cols(a):
        return jnp.concatenate([a] * NSA_GROUP, axis=1)

    def with_ones(vt):
        return jnp.concatenate([vt, jnp.ones((SUBLANE, vt.shape[1]), BF16)], axis=0)

    groups = range(NSA_KV_HEADS)
    lanes = [slice(g * NSA_HEAD_DIM, (g + 1) * NSA_HEAD_DIM) for g in groups]
    for g in groups:
        q4t = jnp.concatenate(
            [qt[(g * NSA_GROUP + r) * NSA_HEAD_DIM:(g * NSA_GROUP + r + 1) * NSA_HEAD_DIM]
             for r in range(NSA_GROUP)], axis=1).astype(BF16)
        for slot in range(SLC_TILES_PER_STEP):
            q4t_scr[slot * NSA_KV_HEADS + g] = jnp.concatenate(
                [q4t, zero] if g == 0 else [zero, q4t], axis=0)
    acc_scr[...] = jnp.zeros(acc_scr.shape, F32)

    own_lanes = [(lax.broadcasted_iota(jnp.int32, (1, KV_W), 1) // NSA_HEAD_DIM) == g for g in groups]
    pad_rows = jnp.zeros((BF16_ROWS - bpt, cols), BF16)

    m_scr[...] = jnp.full(m_scr.shape, NEG, F32)

    def scores(j, g, diagonal, slot):
        k0 = pl.multiple_of(j * tk, tk)
        sb = selt_ref[0, g, pl.ds(pl.multiple_of(j * bpt, bpt), bpt), :]
        og = 1 - g
        qs = slot * NSA_KV_HEADS + g
        q4t_scr[qs, og * NSA_HEAD_DIM:og * NSA_HEAD_DIM + BF16_ROWS, :] = jnp.concatenate(
            [head_cols(sb).astype(BF16), pad_rows], axis=0)
        s = _mm(jnp.where(own_lanes[g], ks_ref[0, pl.ds(k0, tk), :], e0t_ref[g]), q4t_scr[qs])
        if diagonal:
            kpos = k0 + lax.broadcasted_iota(jnp.int32, (tk, 1), 0)
            s = jnp.where(kpos <= head_cols(t), s, NEG)
        return s

    def absorb(j, g, s):
        k0 = pl.multiple_of(j * tk, tk)
        m_old = m_scr[g]
        m_new = jnp.maximum(m_old, jnp.max(s, axis=0, keepdims=True))
        m_scr[g] = m_new
        p = jnp.exp2(s - m_new).astype(BF16)
        acc_scr[g] = (jnp.exp2(m_old - m_new) * acc_scr[g]
                      + _mm(with_ones(vs_ref[0, lanes[g], pl.ds(k0, tk)]), p))

    def run(tiles):
        items = [(j, g, d, slot) for slot, (j, d) in enumerate(tiles) for g in groups]
        ahead = 4
        pending = [scores(*it) for it in items[:ahead]]
        for k, (j, g, _, _) in enumerate(items):
            s = pending.pop(0)
            absorb(j, g, s)
            if k + ahead < len(items):
                pending.append(scores(*items[k + ahead]))

    n_past = n_kt - 1

    def full_step(i, carry):
        run([(SLC_TILES_PER_STEP * i + k, False) for k in range(SLC_TILES_PER_STEP)])
        return carry

    lax.fori_loop(0, n_past // SLC_TILES_PER_STEP, full_step, 0)
    for rem in range(SLC_TILES_PER_STEP):
        def tail(rem=rem):
            run([(n_past - rem + k, False) for k in range(rem)] + [(n_past, True)])
        pl.when(n_past % SLC_TILES_PER_STEP == rem)(tail)

    kw_rows = kw_ref[0, pl.ds(w0, wlen), :]
    for g in groups:
        acc = acc_scr[g]
        o_slc = acc[0:NSA_HEAD_DIM] / acc[NSA_HEAD_DIM:NSA_HEAD_DIM + 1]
        s = _mm(jnp.where(own_lanes[g], kw_rows, jnp.zeros_like(kw_rows)), q4t_scr[g]) + head_cols(wbias)
        p = jnp.exp2(s - jnp.max(s, axis=0, keepdims=True)).astype(BF16)
        acc = _mm(with_ones(vw_ref[0, lanes[g], pl.ds(w0, wlen)]), p)
        o_win = acc[0:NSA_HEAD_DIM] / acc[NSA_HEAD_DIM:NSA_HEAD_DIM + 1]

        for r in range(NSA_GROUP):
            h = g * NSA_GROUP + r
            c = GATE_LANE0 + 3 * h
            cs = slice(r * tq, (r + 1) * tq)
            comb_scr[h * NSA_HEAD_DIM:(h + 1) * NSA_HEAD_DIM, :] = (
                o_slc[:, cs] * gate_t[c + 1:c + 2] + o_win[:, cs] * gate_t[c + 2:c + 3])
    o_ref[...] = ocmp_ref[...] + comb_scr[...].T


def _slc_prompt(nq, ocmp, misc, selt, ks, vs, kw, vw, *, batch, seq, tq, tk):
    nt = seq // tq
    nc = seq // SEL_BLOCK
    bpt = tk // SEL_BLOCK
    key_blk = jnp.arange(tk, dtype=jnp.int32)[None, :, None] // SEL_BLOCK
    other0 = (1 - jnp.arange(NSA_KV_HEADS, dtype=jnp.int32))[:, None, None] * NSA_HEAD_DIM
    e0t = (jnp.arange(KV_W, dtype=jnp.int32)[None, None, :] - other0 == key_blk).astype(BF16)
    assert tk % tq == 0 and seq >= WINDOW + tq and bpt <= BF16_ROWS
    cols = NSA_GROUP * tq
    row = lambda n: pl.BlockSpec((tq, n), lambda b, i: (b * nt + i, 0))
    k_rows = pl.BlockSpec((1, seq, KV_W), lambda b, i: (b, 0, 0), pipeline_mode=pl.Buffered(1))
    v_cols = pl.BlockSpec((1, KV_W, seq), lambda b, i: (b, 0, 0), pipeline_mode=pl.Buffered(1))
    return pl.pallas_call(
        functools.partial(_slc_kernel, tq=tq, tk=tk),
        grid=(batch, nt),
        in_specs=[row(NSA_WIDTH), row(NSA_WIDTH), row(LANE),
                  pl.BlockSpec((1, NSA_KV_HEADS, nc, tq), lambda b, i: (b, 0, 0, i)),
                  _const_spec((NSA_KV_HEADS, tk, KV_W)), k_rows, v_cols, k_rows, v_cols],
        out_specs=row(NSA_WIDTH),
        out_shape=jax.ShapeDtypeStruct((batch * seq, NSA_WIDTH), F32),
        scratch_shapes=[pltpu.VMEM((SLC_TILES_PER_STEP * NSA_KV_HEADS, KV_W, cols), BF16),
                        pltpu.VMEM((NSA_KV_HEADS, AUG_ROWS, cols), F32),
                        pltpu.VMEM((NSA_KV_HEADS, 1, cols), F32),
                        pltpu.VMEM((NSA_WIDTH, tq), F32)],
        compiler_params=pltpu.CompilerParams(dimension_semantics=("arbitrary", "arbitrary"),
                                             vmem_limit_bytes=VMEM_LIMIT),
        name="nsa_slc_prompt",
    )(nq, ocmp, misc, selt, e0t, ks.reshape(batch, seq, KV_W), vs, kw.reshape(batch, seq, KV_W), vw)


PAGES_PER_MM = SUBLANE // (PAGE_SIZE // CMP_BLOCK)


def _pool_cmp_kernel(pk_ref, pv_ref, wk_ref, wv_ref, sel_ref, ok_ref, ov_ref, *, pp):
    def quad(i, carry):
        for p_ref, w_ref, o_ref in ((pk_ref, wk_ref, ok_ref), (pv_ref, wv_ref, ov_ref)):
            acc = jnp.zeros((SUBLANE, KV_W), F32)
            for k in range(PAGES_PER_MM):
                y = (p_ref[i * PAGES_PER_MM + k].reshape(KV_W, PAGE_SIZE) * w_ref[...]).astype(BF16)
                acc = acc + _nt(sel_ref[k], y)
            o_ref[pl.ds(pl.multiple_of(i * SUBLANE, SUBLANE), SUBLANE), :] = acc
        return carry

    lax.fori_loop(0, pp // PAGES_PER_MM, quad, 0, unroll=8)


def _pool_compress(pool_k, pool_v, wck_t, wcv_t, *, pp):
    n_pool = pool_k.shape[0]
    bpp = PAGE_SIZE // CMP_BLOCK
    rowid = jnp.arange(SUBLANE, dtype=jnp.int32)[None, :, None]
    want = (bpp * jnp.arange(PAGES_PER_MM, dtype=jnp.int32)[:, None, None]
            + jnp.arange(PAGE_SIZE, dtype=jnp.int32)[None, None, :] // CMP_BLOCK)
    sel = (rowid == want).astype(BF16)
    pg = pl.BlockSpec((pp, NSA_KV_HEADS, NSA_HEAD_DIM, PAGE_SIZE), lambda i: (i, 0, 0, 0))
    og = pl.BlockSpec((pp * bpp, KV_W), lambda i: (i, 0))
    return pl.pallas_call(
        functools.partial(_pool_cmp_kernel, pp=pp),
        grid=(n_pool // pp,),
        in_specs=[pg, pg, _const_spec((KV_W, PAGE_SIZE)), _const_spec((KV_W, PAGE_SIZE)),
                  _const_spec(sel.shape)],
        out_specs=[og, og],
        out_shape=[jax.ShapeDtypeStruct((n_pool * bpp, KV_W), F32)] * 2,
        compiler_params=pltpu.CompilerParams(dimension_semantics=("arbitrary",),
                                             vmem_limit_bytes=VMEM_LIMIT),
        name="pool_compress",
    )(pool_k, pool_v, wck_t, wcv_t, sel)


def _cmp_step_kernel(pt_ref, q_ref, kcn_ref, vcn_ref, gate_ref, wk_ref, wv_ref, pkc_ref, pvc_ref,
                     o_ref, idx_ref, kbuf, vbuf, *, n_pages, ncp, bb):
    b0 = pl.program_id(0) * bb
    bpp = PAGE_SIZE // CMP_BLOCK
    n_past = n_pages * bpp
    t = n_pages * PAGE_SIZE
    blk = lax.broadcasted_iota(jnp.int32, (1, ncp), 1)
    cmask = blk * CMP_BLOCK + (CMP_BLOCK - 1) <= t
    cur = t // SEL_BLOCK
    forced = (blk == 0) | (blk == cur) | (blk == cur - 1)
    scores = []
    for i in range(bb):
        def gather(j, carry):
            p = pt_ref[b0 + i, j]
            for src, dst in ((pkc_ref, kbuf), (pvc_ref, vbuf)):
                for u in range(bpp):
                    dst[i, pl.ds(j * bpp + u, 1), :] = src[pl.ds(p * bpp + u, 1), :]
            return carry

        lax.fori_loop(0, n_pages, gather, 0)
        kbuf[i, n_past:ncp, :] = jnp.zeros((ncp - n_past, KV_W), F32)
        vbuf[i, n_past:ncp, :] = jnp.zeros((ncp - n_past, KV_W), F32)
        kbuf[i, n_past:n_past + 1, :] = kcn_ref[i:i + 1, :] * wk_ref[0:1, :]
        vbuf[i, n_past:n_past + 1, :] = vcn_ref[i:i + 1, :] * wv_ref[0:1, :]
        kc = kbuf[i].astype(BF16)
        vc = vbuf[i].astype(BF16)
        qrow = q_ref[i:i + 1, :]
        for g in range(NSA_KV_HEADS):
            gl = slice(g * NSA_HEAD_DIM, (g + 1) * NSA_HEAD_DIM)
            p = _masked_softmax(_nt(_group_rows(qrow, g), kc[:, gl]), cmask)
            o4 = _mm(p.astype(BF16), vc[:, gl])
            for r in range(NSA_GROUP):
                h = g * NSA_GROUP + r
                c = GATE_LANE0 + 3 * h
                o_ref[i:i + 1, h * NSA_HEAD_DIM:(h + 1) * NSA_HEAD_DIM] = (
                    o4[r:r + 1] * gate_ref[i:i + 1, c:c + 1])
            score = jnp.where(forced, float(NSA_GROUP + 1),
                              jnp.sum(p[0:NSA_GROUP], axis=0, keepdims=True))
            score = jnp.where(blk > cur, -1.0, score)
            scores.append(jnp.where(blk > n_past, -3.0, score))
    score = jnp.concatenate(scores, axis=0)
    lane = lax.broadcasted_iota(jnp.int32, (1, LANE), 1)
    idx = jnp.zeros((bb * NSA_KV_HEADS, LANE), jnp.int32)
    for n in range(TOP_N):
        j = _first_max(score, blk)
        idx = jnp.where(lane == n, j, idx)
        score = jnp.where(blk == j, -4.0, score)
    idx_ref[...] = idx


def _cmp_step(page_table, nq, kcn, vcn, misc, wck, wcv, pkc, pvc, *, bb):
    n, n_pages = page_table.shape
    bpp = PAGE_SIZE // CMP_BLOCK
    ncp = -(-(n_pages * bpp + 1) // LANE) * LANE
    row = lambda w: pl.BlockSpec((bb, w), lambda i, pt: (i, 0))
    cst = lambda shape: pl.BlockSpec(shape, lambda i, pt: (0, 0), pipeline_mode=pl.Buffered(1))
    return pl.pallas_call(
        functools.partial(_cmp_step_kernel, n_pages=n_pages, ncp=ncp, bb=bb),
        grid_spec=pltpu.PrefetchScalarGridSpec(
            num_scalar_prefetch=1, grid=(n // bb,),
            in_specs=[row(NSA_WIDTH), row(KV_W), row(KV_W), row(LANE),
                      cst((CMP_BLOCK, KV_W)), cst((CMP_BLOCK, KV_W)),
                      cst(pkc.shape), cst(pvc.shape)],
            out_specs=[row(NSA_WIDTH), pl.BlockSpec((bb * NSA_KV_HEADS, LANE), lambda i, pt: (i, 0))],
            scratch_shapes=[pltpu.VMEM((bb, ncp, KV_W), F32), pltpu.VMEM((bb, ncp, KV_W), F32)]),
        out_shape=[jax.ShapeDtypeStruct((n, NSA_WIDTH), F32),
                   jax.ShapeDtypeStruct((n * NSA_KV_HEADS, LANE), jnp.int32)],
        compiler_params=pltpu.CompilerParams(dimension_semantics=("arbitrary",),
                                             vmem_limit_bytes=VMEM_LIMIT),
        name="nsa_cmp_step",
    )(page_table, nq.astype(F32), kcn, vcn, misc, wck, wcv, pkc, pvc)


def _slc_step_kernel(pt_ref, idx_ref, q_ref, ocmp_ref, gate_ref, ksn_ref, vsn_ref, kwn_ref, vwn_ref,
                     kwb_ref, vwb_ref, pks_ref, pvs_ref, o_ref, kbuf, vbuf, sem, *, n_pages, wb):
    b = pl.program_id(0)
    bpp = PAGE_SIZE // SEL_BLOCK
    n_past = n_pages * bpp
    t = n_pages * PAGE_SIZE
    n_sel = NSA_KV_HEADS * TOP_N

    def copies(i):
        page = pt_ref[b, jnp.minimum(idx_ref[b, i], n_past - 1) // bpp]
        g = i // TOP_N
        return (pltpu.make_async_copy(pks_ref.at[page, g], kbuf.at[i], sem.at[0, i]),
                pltpu.make_async_copy(pvs_ref.at[page, g], vbuf.at[i], sem.at[1, i]))

    for i in range(n_sel):
        for cp in copies(i):
            cp.start()
    for i in range(n_sel):
        for cp in copies(i):
            cp.wait()

    kidx = lax.broadcasted_iota(jnp.int32, (1, TOP_N * PAGE_SIZE), 1)
    krow = kidx % PAGE_SIZE
    wpos = t - wb + lax.broadcasted_iota(jnp.int32, (1, wb), 1)
    wmask = (wpos >= 0) & (wpos <= t) & (t - wpos < WINDOW)
    qrow = q_ref[0].astype(F32)

    def with_new_key(s_old, mask_old, vt_old, k_new, v_new, q4, use_new):
        sn = jnp.sum(q4.astype(F32) * k_new.astype(BF16).astype(F32), axis=-1, keepdims=True)
        sn = jnp.where(use_new, sn, NEG)
        so = jnp.where(mask_old, s_old, NEG)
        m = jnp.maximum(jnp.max(so, axis=-1, keepdims=True), sn)
        eo = jnp.where(mask_old, jnp.exp(so - m), 0.0)
        en = jnp.where(use_new, jnp.exp(sn - m), 0.0)
        den = jnp.maximum(jnp.sum(eo, axis=-1, keepdims=True) + en, 1e-30)
        return (_nt((eo / den).astype(BF16), vt_old)
                + (en / den).astype(BF16).astype(F32) * v_new.astype(BF16).astype(F32))

    for g in range(NSA_KV_HEADS):
        gl = slice(g * NSA_HEAD_DIM, (g + 1) * NSA_HEAD_DIM)
        q4 = _group_rows(qrow, g)
        kblk = jnp.zeros((1, TOP_N * PAGE_SIZE), jnp.int32)
        has_new = False
        for n in range(TOP_N):
            blk = idx_ref[b, g * TOP_N + n]
            has_new = jnp.logical_or(has_new, blk >= n_past)
            kblk = jnp.where(kidx // PAGE_SIZE == n, blk, kblk)
        valid = ((krow // SEL_BLOCK == kblk % bpp) & (kblk < n_past)
                 & ((kblk // bpp) * PAGE_SIZE + krow <= t))
        kt = jnp.concatenate([kbuf[g * TOP_N + n] for n in range(TOP_N)], axis=1).astype(BF16)
        vt = jnp.concatenate([vbuf[g * TOP_N + n] for n in range(TOP_N)], axis=1).astype(BF16)
        o_slc = with_new_key(_mm(q4, kt), valid, vt, ksn_ref[0, :, gl], vsn_ref[0, :, gl], q4, has_new)
        o_win = with_new_key(_mm(q4, kwb_ref[0, g].astype(BF16)), wmask, vwb_ref[0, g].astype(BF16),
                             kwn_ref[0, :, gl], vwn_ref[0, :, gl], q4, True)
        for r in range(NSA_GROUP):
            h = g * NSA_GROUP + r
            hl = slice(h * NSA_HEAD_DIM, (h + 1) * NSA_HEAD_DIM)
            c = GATE_LANE0 + 3 * h
            o_ref[0, :, hl] = (ocmp_ref[0, :, hl] + o_slc[r:r + 1] * gate_ref[0, :, c + 1:c + 2]
                               + o_win[r:r + 1] * gate_ref[0, :, c + 2:c + 3])


def _slc_step(page_table, idx, nq, ocmp, misc, ksn, vsn, kwn, vwn, buf_kw, buf_vw, pool_ks, pool_vs):
    n, n_pages = page_table.shape
    wb = buf_kw.shape[-1]
    n_sel = NSA_KV_HEADS * TOP_N
    r3 = lambda a: a.reshape(n, 1, a.shape[-1])
    one = lambda w: pl.BlockSpec((1, 1, w), lambda i, pt, ix: (i, 0, 0))
    win = pl.BlockSpec((1, NSA_KV_HEADS, NSA_HEAD_DIM, wb), lambda i, pt, ix: (i, 0, 0, 0))
    hbm = pl.BlockSpec(memory_space=pl.ANY)
    return pl.pallas_call(
        functools.partial(_slc_step_kernel, n_pages=n_pages, wb=wb),
        grid_spec=pltpu.PrefetchScalarGridSpec(
            num_scalar_prefetch=2, grid=(n,),
            in_specs=[one(NSA_WIDTH), one(NSA_WIDTH), one(LANE), one(KV_W), one(KV_W), one(KV_W),
                      one(KV_W), win, win, hbm, hbm],
            out_specs=one(NSA_WIDTH),
            scratch_shapes=[pltpu.VMEM((n_sel, NSA_HEAD_DIM, PAGE_SIZE), F32),
                            pltpu.VMEM((n_sel, NSA_HEAD_DIM, PAGE_SIZE), F32),
                            pltpu.SemaphoreType.DMA((2, n_sel))]),
        out_shape=jax.ShapeDtypeStruct((n, 1, NSA_WIDTH), F32),
        compiler_params=pltpu.CompilerParams(dimension_semantics=("arbitrary",),
                                             vmem_limit_bytes=VMEM_LIMIT),
        name="nsa_slc_step",
    )(page_table, idx, r3(nq), r3(ocmp), r3(misc), r3(ksn), r3(vsn), r3(kwn), r3(vwn),
      buf_kw, buf_vw, pool_ks, pool_vs)


def _post_kernel(x_ref, og_ref, on_ref, pe_ref, p0_ref, p1_ref, wo_ref, gf_ref, wup_ref, wcv_ref,
                 bcv_ref, wdn_ref, gp_ref, wple_ref, wpg_ref, gfin_ref, y_ref, ut_ref,
                 carry, ext, act_scr, *, tm, tail, shift):
    attn = (_mm(og_ref[...].astype(BF16), wo_ref[0:GLA_WIDTH, :])
            + _mm(on_ref[...].astype(BF16), wo_ref[GLA_WIDTH:, :]))
    h = x_ref[...] + attn
    hn = _rms(h, gf_ref[...]).astype(BF16)
    if shift:
        @pl.when(pl.program_id(1) == 0)
        def _():
            carry[...] = jnp.zeros_like(carry)

    for j in range(N_FF_CHUNKS):
        u = _mm(hn, wup_ref[j])
        wc = wcv_ref[j]
        if shift:
            e = j % 2
            ext[e, 0:SUBLANE, :] = carry[j]
            ext[e, SUBLANE:, :] = u
            um2 = ext[e, pl.ds(SUBLANE - 2, tm), :]
            um1 = ext[e, pl.ds(SUBLANE - 1, tm), :]
            carry[j] = u[tm - SUBLANE:, :]
        else:
            um2 = p0_ref[j]
            um1 = p1_ref[j]
        c = bcv_ref[j] + um2 * wc[0:1] + um1 * wc[1:2] + u * wc[2:3]
        ut_ref[0, j] = u[tm - tail:, :]
        a = c[:, :FF_CHUNK]
        act_scr[:, j * FF_CHUNK:(j + 1) * FF_CHUNK] = (a * _sigmoid(a) * c[:, FF_CHUNK:]).astype(BF16)
    h = h + _mm(act_scr[...], wdn_ref[...])
    gate = _sigmoid(_mm(_rms(h, gp_ref[...]).astype(BF16), wpg_ref[...]))
    h = h + _mm(pe_ref[...].astype(BF16), wple_ref[...]) * gate
    y_ref[...] = _rms(h, gfin_ref[...])


def _post(x2d, ogla, onsa, pe2d, prev, pw, g_final, *, batch, tm, shift):
    m = x2d.shape[0]
    nt = m // (batch * tm)
    tail = SUBLANE if shift else tm
    row = lambda n: pl.BlockSpec((tm, n), lambda b, i: (b * nt + i, 0))
    cst = lambda shape: pl.BlockSpec(shape, lambda b, i: (0,) * len(shape),
                                     pipeline_mode=pl.Buffered(1))
    if shift:
        prev_spec = cst((1, 1, LANE))
        p0 = p1 = jnp.zeros((1, 1, LANE), F32)
    else:
        prev_spec = pl.BlockSpec((N_FF_CHUNKS, tm, 2 * FF_CHUNK), lambda b, i: (0, b * nt + i, 0))
        p0, p1 = prev
    return pl.pallas_call(
        functools.partial(_post_kernel, tm=tm, tail=tail, shift=shift),
        grid=(batch, nt),
        in_specs=[row(D_MODEL), row(GLA_WIDTH), row(NSA_WIDTH), row(PLE_DIM), prev_spec, prev_spec,
                  cst((D_MODEL, D_MODEL)), cst((1, D_MODEL)),
                  cst((N_FF_CHUNKS, D_MODEL, 2 * FF_CHUNK)), cst((N_FF_CHUNKS, CONV_W, 2 * FF_CHUNK)),
                  cst((N_FF_CHUNKS, 1, 2 * FF_CHUNK)), cst((D_FF, D_MODEL)),
                  cst((1, D_MODEL)), cst((PLE_DIM, D_MODEL)), cst((D_MODEL, D_MODEL)),
                  cst((1, D_MODEL))],
        out_specs=[row(D_MODEL),
                   pl.BlockSpec((1, N_FF_CHUNKS, tail, 2 * FF_CHUNK), lambda b, i: (b, 0, 0, 0))],
        out_shape=[jax.ShapeDtypeStruct((m, D_MODEL), F32),
                   jax.ShapeDtypeStruct((batch, N_FF_CHUNKS, tail, 2 * FF_CHUNK), F32)],
        scratch_shapes=[pltpu.VMEM((N_FF_CHUNKS, SUBLANE, 2 * FF_CHUNK), F32),
                        pltpu.VMEM((2, tm + SUBLANE, 2 * FF_CHUNK), F32),
                        pltpu.VMEM((tm, D_FF), BF16)],
        compiler_params=pltpu.CompilerParams(dimension_semantics=("arbitrary", "arbitrary"),
                                             vmem_limit_bytes=VMEM_LIMIT),
        name="post",
    )(x2d, ogla, onsa, pe2d, p0, p1, pw["w_o"], pw["g_ffn"], pw["w_up"], pw["w_conv"], pw["b_conv"],
      pw["w_down"], pw["g_ple"], pw["w_ple"], pw["w_ple_gate"], g_final)


def _ff_chunked(a):
    lead = a.shape[:-1]
    a = a.reshape(*lead, 2, N_FF_CHUNKS, FF_CHUNK)
    a = jnp.moveaxis(a, -2, 0)
    return a.reshape(N_FF_CHUNKS, *lead, 2 * FF_CHUNK)


def _ff_unchunked(a):
    r = a.shape[1]
    return a.reshape(N_FF_CHUNKS, r, 2, FF_CHUNK).transpose(1, 2, 0, 3).reshape(r, 2 * D_FF)


def _pack_weights(i, g_attn, w_in, w_gla_gate, b_gla_gate, g_gla_out, b_nsa_gate, w_cmp_k, w_cmp_v,
                  w_o, g_ffn, w_up, w_conv, b_conv, w_down, g_ple, w_ple, w_ple_gate):
    w = w_in[i]
    o_glr = 2 * GLA_QK + 2 * GLA_WIDTH
    o_nq = o_glr + GLA_GATE_RANK
    o_ng = o_nq + NSA_WIDTH + 6 * KV_W
    pad = jnp.zeros((D_MODEL, LANE - GLA_GATE_RANK - 3 * NSA_HEADS), w.dtype)
    w_pack = jnp.concatenate([w[:, :o_glr], w[:, o_nq:o_ng], w[:, o_glr:o_nq], w[:, o_ng:], pad], axis=1)
    w_gate = jnp.zeros((LANE, GLA_QK), F32).at[:GLA_GATE_RANK].set(w_gla_gate[i])
    b_misc = jnp.zeros((1, LANE), F32).at[0, GATE_LANE0:GATE_LANE0 + 3 * NSA_HEADS].set(b_nsa_gate[i])
    return {
        "g_attn": g_attn[i][None], "w_in": w_pack.astype(BF16), "w_gate": w_gate.astype(BF16),
        "b_gate": b_gla_gate[i][None], "b_misc": b_misc, "g_gla_out": g_gla_out[i][None],
        "w_ck": jnp.repeat(w_cmp_k[i], NSA_HEAD_DIM, axis=1),
        "w_cv": jnp.repeat(w_cmp_v[i], NSA_HEAD_DIM, axis=1),
        "w_ck_t": jnp.tile(jnp.repeat(w_cmp_k[i].T, NSA_HEAD_DIM, axis=0), (1, PAGE_SIZE // CMP_BLOCK)),
        "w_cv_t": jnp.tile(jnp.repeat(w_cmp_v[i].T, NSA_HEAD_DIM, axis=0), (1, PAGE_SIZE // CMP_BLOCK)),
        "w_o": w_o[i].astype(BF16), "g_ffn": g_ffn[i][None],
        "w_up": _ff_chunked(w_up[i]).astype(BF16), "w_conv": _ff_chunked(w_conv[i]),
        "b_conv": _ff_chunked(b_conv[i][None]),
        "w_down": w_down[i].astype(BF16),
        "g_ple": g_ple[i][None], "w_ple": w_ple[i].astype(BF16),
        "w_ple_gate": w_ple_gate[i].astype(BF16),
    }


def _rope_tables(pos):
    half = ROT_DIM // 2
    inv = ROPE_THETA ** (-jnp.arange(half, dtype=F32) / half)
    ang = pos.astype(F32)[:, None] * inv[None, :]
    d = jnp.arange(LANE) % NSA_HEAD_DIM
    cos = jnp.cos(ang)[:, d % half]
    sin = jnp.sin(ang)[:, d % half]
    return (jnp.where(d < ROT_DIM, cos, 1.0), jnp.where(d < half, -sin, 0.0),
            jnp.where((d >= half) & (d < ROT_DIM), sin, 0.0))


def _kv4(a, b, t):
    return a.reshape(b, t, NSA_KV_HEADS, NSA_HEAD_DIM)


def _prompt_layer(x, pe, pw, g_final):
    b, t, _ = x.shape
    m = b * t
    tabs = _rope_tables(jnp.arange(t, dtype=jnp.int32))
    (gqk, gv, gr, la, nq, misc, kc, vc, ks, vs, kw, vw, ks16, vs16, kw16, vw16, kcc, vcc) = _inproj(
        x.reshape(m, D_MODEL), tabs, pw, tm=512, batch=b)
    ogla, h_gla = _gla_prompt(gqk, gv, gr, la, pw["g_gla_out"], batch=b, seq=t, tg=512)
    ocmp, selt = _cmp_prompt(nq, kcc, vcc, misc, batch=b, seq=t, tq=256)
    onsa = _slc_prompt(nq, ocmp, misc, selt, ks16, vs16, kw16, vw16, batch=b, seq=t, tq=128, tk=512)
    y, utail = _post(x.reshape(m, D_MODEL), ogla, onsa, pe.reshape(m, PLE_DIM), None, pw, g_final,
                     batch=b, tm=512, shift=True)
    conv_new = jax.vmap(_ff_unchunked)(utail)[:, SUBLANE - (CONV_W - 1):]
    nw = min(WINDOW, t)

    def rows(a):
        return a.reshape(b, NSA_KV_HEADS, NSA_HEAD_DIM, t).transpose(0, 3, 1, 2)

    state = (rows(kc), rows(vc), rows(ks), rows(vs), rows(kw)[:, t - nw:], rows(vw)[:, t - nw:],
             h_gla, conv_new)
    return y.reshape(b, t, D_MODEL), state


def _sample_layer(x, pe, pw, g_final, pools, bufs, h0, conv_prev, page_table):
    n, t, _ = x.shape
    assert t == 1, "the sample group is written for one new token per request"
    n_pages = page_table.shape[1]
    pos = jnp.full((n,), n_pages * PAGE_SIZE, jnp.int32)
    tabs = _rope_tables(pos)
    x2d = x.reshape(n, D_MODEL)
    (gqk, gv, gr, la, nq, misc, kc, vc, ks, vs, kw, vw) = _inproj(x2d, tabs, pw, tm=n)
    ogla, h_gla = _gla_step(gqk, gv, gr, la, pw["g_gla_out"], h0, bb=SUBLANE)
    pool_kc, pool_vc, pool_ks, pool_vs = [p.transpose(0, 2, 3, 1) for p in pools]
    n_pool = pool_kc.shape[0]
    pp = max(p for p in range(PAGES_PER_MM, 65, PAGES_PER_MM) if n_pool % p == 0)
    pkc, pvc = _pool_compress(pool_kc, pool_vc, pw["w_ck_t"], pw["w_cv_t"], pp=pp)
    ocmp, idx = _cmp_step(page_table, nq, kc, vc, misc, pw["w_ck"], pw["w_cv"], pkc, pvc, bb=SUBLANE)
    idx = idx.reshape(n, NSA_KV_HEADS, LANE)[:, :, :TOP_N].reshape(n, NSA_KV_HEADS * TOP_N)
    buf_kw, buf_vw = bufs
    onsa = _slc_step(page_table, idx, nq, ocmp, misc, ks, vs, kw, vw,
                     buf_kw.transpose(0, 2, 3, 1), buf_vw.transpose(0, 2, 3, 1), pool_ks, pool_vs)
    prev = (_ff_chunked(conv_prev[:, 0]), _ff_chunked(conv_prev[:, 1]))
    y, utail = _post(x2d, ogla, onsa.reshape(n, NSA_WIDTH), pe.reshape(n, PLE_DIM), prev, pw, g_final,
                     batch=1, tm=n, shift=False)
    u = _ff_unchunked(utail[0])
    conv_new = jnp.stack([conv_prev[:, 1], u], axis=1)
    wb = buf_kw.shape[1]
    nw = min(WINDOW, wb + 1)
    kwin = jnp.concatenate([buf_kw, _kv4(kw, n, 1)], axis=1)[:, wb + 1 - nw:]
    vwin = jnp.concatenate([buf_vw, _kv4(vw, n, 1)], axis=1)[:, wb + 1 - nw:]
    state = (_kv4(kc, n, 1), _kv4(vc, n, 1), _kv4(ks, n, 1), _kv4(vs, n, 1), kwin, vwin, h_gla,
             conv_new)
    return y.reshape(n, 1, D_MODEL), state


def kernel(x_prompt, x_sample, p_prompt, p_sample, cache_k_cmp, cache_v_cmp, cache_k_slc, cache_v_slc, cache_k_win, cache_v_win, state_gla, state_conv, page_table, g_attn, w_in, w_gla_gate, b_gla_gate, g_gla_out, b_nsa_gate, w_cmp_k, w_cmp_v, w_o, g_ffn, w_up, w_conv, b_conv, w_down, g_ple, w_ple, w_ple_gate, g_final):
    depth = w_in.shape[0]
    assert depth == 1, "the final norm is fused into the layer kernel; written for a one-layer trunk"
    pw = _pack_weights(0, g_attn, w_in, w_gla_gate, b_gla_gate, g_gla_out, b_nsa_gate, w_cmp_k,
                       w_cmp_v, w_o, g_ffn, w_up, w_conv, b_conv, w_down, g_ple, w_ple, w_ple_gate)
    gfin = g_final[None]
    y_p, st_p = _prompt_layer(x_prompt, p_prompt[0], pw, gfin)
    y_s, st_s = _sample_layer(
        x_sample, p_sample[0], pw, gfin,
        (cache_k_cmp[0], cache_v_cmp[0], cache_k_slc[0], cache_v_slc[0]),
        (cache_k_win[0], cache_v_win[0]), state_gla[0], state_conv[0], page_table)
    return (y_p, y_s, *[a[None] for a in st_p], *[a[None] for a in st_s])
```
